```python
import math
import jax, jax.numpy as jnp
from jax import lax
import numpy as np

D_MODEL = 2048
BATCH = 4
SEQ = 2048
DEPTH = 1

MIX_WIDTH = D_MODEL
DIFF_WIDTH = MIX_WIDTH // 2
DIFF_HEAD_DIM = 64
DIFF_V_DIM = 2 * DIFF_HEAD_DIM
DIFF_HEADS = DIFF_WIDTH // DIFF_V_DIM
DN_WIDTH = MIX_WIDTH - DIFF_WIDTH
DN_HEAD_DIM = 128
DN_HEADS = DN_WIDTH // DN_HEAD_DIM
CONV_K = 4
CHUNK = 64
Q_BLOCK = 128
N_BUCKETS = 32
MAX_DISTANCE = 128
N_EXPERTS = 32
TOP_K = 4
D_EXPERT = D_MODEL
SWIGLU_LIMIT = 7.0
SWIGLU_ALPHA = 1.702
MOE_BLOCK = 128
EPS = 1e-6

IN_SIZES = [
    DIFF_HEADS * 2 * DIFF_HEAD_DIM,
    DIFF_HEADS * 2 * DIFF_HEAD_DIM,
    DIFF_WIDTH,
    DN_WIDTH,
    DN_WIDTH,
    DN_WIDTH,
    DN_WIDTH,
    DN_HEADS,
    DN_HEADS,
]
IN_COLS = sum(IN_SIZES)
IN_SPLITS = [int(s) for s in np.cumsum(IN_SIZES)[:-1]]

kernel_name = "hybrid_diffattn_gdn_moe_adaln"


def rmsnorm(x, g):
    x32 = x.astype(jnp.float32)
    y = x32 * lax.rsqrt(jnp.mean(x32 * x32, axis=-1, keepdims=True) + EPS)
    return (y * g.astype(jnp.float32)).astype(x.dtype)


def l2norm(x):
    return x * lax.rsqrt(jnp.sum(x * x, axis=-1, keepdims=True) + EPS)


def t5_bucket(n):
    n = jnp.maximum(n, 0)
    max_exact = N_BUCKETS // 2
    nf = jnp.maximum(n, max_exact).astype(jnp.float32)
    large = max_exact + (jnp.log(nf / max_exact) / math.log(MAX_DISTANCE / max_exact)
                         * (N_BUCKETS - max_exact)).astype(jnp.int32)
    large = jnp.minimum(large, N_BUCKETS - 1)
    return jnp.where(n < max_exact, n, large)


def diff_attention(q, k, v, rel_bias, lam, lambda_init, g_subln):
    B, S, H = q.shape[0], q.shape[1], q.shape[2]
    nb = S // Q_BLOCK
    qb = q.reshape(B, nb, Q_BLOCK, H, 2, DIFF_HEAD_DIM).transpose(1, 0, 2, 3, 4, 5)
    k_pos = jnp.arange(S, dtype=jnp.int32)
    scale = DIFF_HEAD_DIM ** -0.5

    def block(args):
        q_blk, i = args
        q_pos = i * Q_BLOCK + jnp.arange(Q_BLOCK, dtype=jnp.int32)
        rel = q_pos[:, None] - k_pos[None, :]
        bias = rel_bias.astype(jnp.float32)[t5_bucket(rel)].transpose(2, 0, 1)
        s = jnp.einsum('bqhmd,bkhmd->bmhqk', q_blk, k).astype(jnp.float32) * scale + bias
        s = jnp.where(rel >= 0, s, jnp.float32(-1e30))
        p = jax.nn.softmax(s, axis=-1)
        w = p[:, 0] - lam * p[:, 1]
        return jnp.einsum('bhqk,bkhd->bqhd', w.astype(v.dtype), v)

    o = lax.map(block, (qb, jnp.arange(nb, dtype=jnp.int32)))
    o = o.transpose(1, 0, 2, 3, 4).reshape(B, S, H, DIFF_V_DIM)
    return rmsnorm(o, g_subln) * (1.0 - lambda_init)


def causal_depthwise_conv(u, w):
    K, C = w.shape
    return lax.conv_general_dilated(u, w[:, None, :].astype(u.dtype), window_strides=(1,),
                                    padding=[(K - 1, 0)],
                                    dimension_numbers=('NWC', 'WIO', 'NWC'),
                                    feature_group_count=C)


def gated_delta_rule_chunked(q, k, v, g, beta):
    B, S, H, Dk = q.shape
    Dv = v.shape[-1]
    C = CHUNK
    N = S // C
    q = q * (Dk ** -0.5)

    def chunks(t):
        return t.reshape(B, N, C, H, -1).transpose(0, 3, 1, 2, 4)
    q, k, v = chunks(q), chunks(k), chunks(v)
    beta = beta.reshape(B, N, C, H).transpose(0, 3, 1, 2)
    g = jnp.cumsum(g.reshape(B, N, C, H).transpose(0, 3, 1, 2), axis=-1)
    tril = jnp.tril(jnp.ones((C, C), dtype=bool))
    strict = jnp.tril(jnp.ones((C, C), dtype=bool), k=-1)
    decay = jnp.exp(jnp.where(tril, g[..., :, None] - g[..., None, :], -jnp.inf))
    k_beta = k * beta[..., None]
    v_beta = v * beta[..., None]
    L = jnp.where(strict, jnp.einsum('bhncd,bhnjd->bhncj', k_beta, k) * decay, 0.0)
    A = L + jnp.eye(C, dtype=L.dtype)
    rhs = jnp.concatenate([v_beta, k_beta * jnp.exp(g)[..., None]], axis=-1)
    sol = lax.linalg.triangular_solve(A, rhs, left_side=True, lower=True, unit_diagonal=True)
    u, w = sol[..., :Dv], sol[..., Dv:]
    attn = jnp.einsum('bhncd,bhnjd->bhncj', q, k) * decay
    q_dec = q * jnp.exp(g)[..., None]
    k_tail = k * jnp.exp(g[..., -1:] - g)[..., None]
    g_last = jnp.exp(g[..., -1])

    def to_front(t):
        return jnp.moveaxis(t, 2, 0)

    def step(state, inp):
        u_n, w_n, attn_n, qd_n, kt_n, gl_n = inp
        v_new = u_n - jnp.einsum('bhck,bhkv->bhcv', w_n, state)
        out = jnp.einsum('bhck,bhkv->bhcv', qd_n, state) + jnp.einsum('bhcj,bhjv->bhcv', attn_n, v_new)
        state = state * gl_n[..., None, None] + jnp.einsum('bhck,bhcv->bhkv', kt_n, v_new)
        return state, out

    state0 = jnp.zeros((B, H, Dk, Dv), jnp.float32)
    _, outs = lax.scan(step, state0, (to_front(u), to_front(w), to_front(attn),
                                      to_front(q_dec), to_front(k_tail), to_front(g_last)))
    return outs.transpose(1, 0, 3, 2, 4).reshape(B, S, H, Dv)


def moe_ffn(h, l, w_router, b_router, w_gate_up, b_gate_up, w_down, b_down):
    Ntok, D = h.shape
    logits = (h @ w_router[l] + b_router[l]).astype(jnp.float32)
    top_vals, top_idx = lax.top_k(logits, TOP_K)
    gates = jax.nn.softmax(top_vals, axis=-1)
    A = Ntok * TOP_K
    exp_flat = top_idx.reshape(A)
    tok_flat = jnp.arange(A, dtype=jnp.int32) // TOP_K
    gate_flat = gates.reshape(A)
    order = jnp.argsort(exp_flat)
    e_sorted = exp_flat[order]
    counts = jnp.bincount(exp_flat, length=N_EXPERTS)
    starts = jnp.cumsum(counts) - counts
    padded = (counts + MOE_BLOCK - 1) // MOE_BLOCK * MOE_BLOCK
    pad_ends = jnp.cumsum(padded)
    pad_starts = pad_ends - padded
    rank = jnp.arange(A, dtype=jnp.int32) - starts[e_sorted]
    dest = pad_starts[e_sorted] + rank
    n_blocks = -(-A // MOE_BLOCK) + N_EXPERTS
    P = n_blocks * MOE_BLOCK
    row_tok = jnp.full((P,), Ntok, jnp.int32).at[dest].set(tok_flat[order])
    row_gate = jnp.zeros((P,), jnp.float32).at[dest].set(gate_flat[order])
    block_exp = jnp.minimum(jnp.searchsorted(pad_ends, jnp.arange(n_blocks) * MOE_BLOCK, side='right'),
                            N_EXPERTS - 1).astype(jnp.int32)
    h_pad = jnp.concatenate([h, jnp.zeros((1, D), h.dtype)], axis=0)
    xs = h_pad[row_tok].reshape(n_blocks, MOE_BLOCK, D)

    def expert_block(args):
        xb, e = args
        gu = xb @ w_gate_up[l, e] + b_gate_up[l, e]
        gate, up = gu[:, :D_EXPERT], gu[:, D_EXPERT:]
        gate = jnp.minimum(gate, SWIGLU_LIMIT)
        up = jnp.clip(up, -SWIGLU_LIMIT, SWIGLU_LIMIT)
        glu = gate * jax.nn.sigmoid(gate * SWIGLU_ALPHA)
        return ((up + 1.0) * glu) @ w_down[l, e] + b_down[l, e]

    ys = lax.map(expert_block, (xs, block_exp)).reshape(P, D)
    out = jnp.zeros((Ntok + 1, D), h.dtype).at[row_tok].add((ys * row_gate[:, None]).astype(h.dtype))
    return out[:Ntok]


def setup_inputs(seed: int = 0) -> dict:
    key = jax.random.key(seed)
    ks = jax.random.split(key, 26)
    nrm = jax.random.normal
    D, E, F = D_MODEL, N_EXPERTS, D_EXPERT
    dt = jnp.exp(jax.random.uniform(ks[13], (DEPTH, DN_HEADS)) * (math.log(0.1) - math.log(0.001)) + math.log(0.001))
    return {
        "x": nrm(ks[0], (BATCH, SEQ, D)),
        "c": nrm(ks[1], (BATCH, D)),
        "w_ada": nrm(ks[2], (DEPTH, D, 6 * D)) * (0.5 * D ** -0.5),
        "b_ada": nrm(ks[3], (DEPTH, 6 * D)) * 0.01,
        "g_norm_mix": 1.0 + 0.02 * nrm(ks[4], (DEPTH, D)),
        "g_norm_ffn": 1.0 + 0.02 * nrm(ks[5], (DEPTH, D)),
        "g_norm_final": 1.0 + 0.02 * nrm(ks[6], (D,)),
        "w_in": nrm(ks[7], (DEPTH, D, IN_COLS)) * D ** -0.5,
        "conv_w": nrm(ks[8], (DEPTH, CONV_K, 3 * DN_WIDTH)) * CONV_K ** -0.5,
        "rel_bias": nrm(ks[9], (N_BUCKETS, DIFF_HEADS)) * 0.5,
        "lambda_q1": nrm(ks[10], (DEPTH, DIFF_HEAD_DIM)) * 0.1,
        "lambda_k1": nrm(ks[11], (DEPTH, DIFF_HEAD_DIM)) * 0.1,
        "lambda_q2": nrm(ks[12], (DEPTH, DIFF_HEAD_DIM)) * 0.1,
        "lambda_k2": nrm(ks[14], (DEPTH, DIFF_HEAD_DIM)) * 0.1,
        "g_subln": 1.0 + 0.02 * nrm(ks[15], (DEPTH, DIFF_V_DIM)),
        "a_log": jnp.log(jax.random.uniform(ks[16], (DEPTH, DN_HEADS), minval=1.0, maxval=16.0)),
        "dt_bias": dt + jnp.log(-jnp.expm1(-dt)),
        "g_gnorm": 1.0 + 0.02 * nrm(ks[17], (DEPTH, DN_HEAD_DIM)),
        "w_out": nrm(ks[18], (DEPTH, MIX_WIDTH, D)) * MIX_WIDTH ** -0.5,
        "w_router": nrm(ks[19], (DEPTH, D, E)) * D ** -0.5,
        "b_router": nrm(ks[20], (DEPTH, E)) * 0.01,
        "w_gate_up": nrm(ks[21], (DEPTH, E, D, 2 * F)) * D ** -0.5,
        "b_gate_up": nrm(ks[22], (DEPTH, E, 2 * F)) * 0.01,
        "w_down": nrm(ks[23], (DEPTH, E, F, D)) * F ** -0.5,
        "b_down": nrm(ks[24], (DEPTH, E, D)) * 0.01,
    }


def reference(x, c, w_ada, b_ada, g_norm_mix, g_norm_ffn, g_norm_final, w_in, conv_w, rel_bias,
              lambda_q1, lambda_k1, lambda_q2, lambda_k2, g_subln, a_log, dt_bias, g_gnorm, w_out,
              w_router, b_router, w_gate_up, b_gate_up, w_down, b_down):
    B, S, D = x.shape
    for l in range(DEPTH):
        mod = jax.nn.silu(c) @ w_ada[l] + b_ada[l]
        shift1, scale1, gate1, shift2, scale2, gate2 = [m[:, None, :] for m in jnp.split(mod, 6, axis=-1)]

        hn = rmsnorm(x, g_norm_mix[l]) * (1.0 + scale1) + shift1
        proj = hn @ w_in[l]
        dq, dk, dv, nq, nk, nv, ngate, nbeta, na = jnp.split(proj, IN_SPLITS, axis=-1)

        lambda_init = 0.8 - 0.6 * math.exp(-0.3 * l)
        lam = (jnp.exp(jnp.sum(lambda_q1[l] * lambda_k1[l]).astype(jnp.float32))
               - jnp.exp(jnp.sum(lambda_q2[l] * lambda_k2[l]).astype(jnp.float32)) + lambda_init)
        o_diff = diff_attention(dq.reshape(B, S, DIFF_HEADS, 2, DIFF_HEAD_DIM),
                                dk.reshape(B, S, DIFF_HEADS, 2, DIFF_HEAD_DIM),
                                dv.reshape(B, S, DIFF_HEADS, DIFF_V_DIM),
                                rel_bias, lam, lambda_init, g_subln[l])

        qkv = jax.nn.silu(causal_depthwise_conv(jnp.concatenate([nq, nk, nv], axis=-1), conv_w[l]))
        cq, ck, cv = jnp.split(qkv.astype(jnp.float32), 3, axis=-1)
        cq = l2norm(cq.reshape(B, S, DN_HEADS, DN_HEAD_DIM))
        ck = l2norm(ck.reshape(B, S, DN_HEADS, DN_HEAD_DIM))
        cv = cv.reshape(B, S, DN_HEADS, DN_HEAD_DIM)
        beta = jax.nn.sigmoid(nbeta.astype(jnp.float32))
        g = -jnp.exp(a_log[l].astype(jnp.float32)) * jax.nn.softplus(
            na.astype(jnp.float32) + dt_bias[l].astype(jnp.float32))
        o_dn = gated_delta_rule_chunked(cq, ck, cv, g, beta).astype(x.dtype)
        o_dn = rmsnorm(o_dn, g_gnorm[l]) * jax.nn.silu(ngate.reshape(B, S, DN_HEADS, DN_HEAD_DIM))

        mixed = jnp.concatenate([o_diff.reshape(B, S, DIFF_WIDTH), o_dn.reshape(B, S, DN_WIDTH)], axis=-1)
        x = x + gate1 * (mixed @ w_out[l])

        hn2 = rmsnorm(x, g_norm_ffn[l]) * (1.0 + scale2) + shift2
        y = moe_ffn(hn2.reshape(B * S, D), l, w_router, b_router, w_gate_up, b_gate_up, w_down, b_down)
        x = x + gate2 * y.reshape(B, S, D)
    return rmsnorm(x, g_norm_final)
```

```python
import functools
import math

import jax
import jax.numpy as jnp
import numpy as np
from jax import lax
from jax.experimental import pallas as pl
from jax.experimental.pallas import tpu as pltpu

F32 = jnp.float32
BF16 = jnp.bfloat16

D_MODEL = 2048
DIFF_HEADS = 8
DIFF_HEAD_DIM = 64
DIFF_V_DIM = 128
DN_HEADS = 8
DN_HEAD_DIM = 128
HEAD_LANES = 128
CONV_K = 4
N_BUCKETS = 32
MAX_DISTANCE = 128
N_EXPERTS = 32
TOP_K = 4
D_EXPERT = 2048
SWIGLU_LIMIT = 7.0
SWIGLU_ALPHA = 1.702
EPS = 1e-6
LAMBDA_INIT = 0.8 - 0.6 * math.exp(-0.3 * 0)
N_MAIN_GROUPS = 7
MAIN_COLS = N_MAIN_GROUPS * 1024

V7X_VMEM_BYTES = 64 * 1024 * 1024
VMEM_LIMIT = 56 * 1024 * 1024

NEG_BIG = -1e30


def _cparams(sem):
    return pltpu.CompilerParams(dimension_semantics=sem, vmem_limit_bytes=VMEM_LIMIT)


def _split3(a):
    hi = a.astype(BF16)
    r1 = a - hi.astype(F32)
    mid = r1.astype(BF16)
    lo = (r1 - mid.astype(F32)).astype(BF16)
    return hi, mid, lo


def _dot(a, b, dims=None):
    if dims is None:
        return jnp.dot(a, b, preferred_element_type=F32)
    return lax.dot_general(a, b, (dims, ((), ())), preferred_element_type=F32)


def _dot_exact_lhs(a, b_bf16, dims=None):
    hi, mid, lo = _split3(a)
    return (_dot(hi, b_bf16, dims) + _dot(mid, b_bf16, dims)) + _dot(lo, b_bf16, dims)


def _dot_exact_rhs(a_bf16, b, dims=None):
    hi, mid, lo = _split3(b)
    return (_dot(a_bf16, hi, dims) + _dot(a_bf16, mid, dims)) + _dot(a_bf16, lo, dims)


NT = ((1,), (1,))


def _ones_where(mask):
    return jnp.where(mask, 1.0, 0.0).astype(BF16)


ADA_TN = 1024


def _adaln_kernel(c_ref, w_ref, b_ref, o_ref):
    c = c_ref[...]
    sc = c * jax.nn.sigmoid(c)
    a_hi, a_mid, a_lo = _split3(sc)
    w_hi, w_mid, w_lo = _split3(w_ref[...])
    acc = _dot(a_hi, w_hi)
    acc += _dot(a_hi, w_mid) + _dot(a_mid, w_hi)
    acc += _dot(a_hi, w_lo) + _dot(a_mid, w_mid) + _dot(a_lo, w_hi)
    o_ref[...] = acc + b_ref[...]


def _adaln(c_pad, w, b):
    n = w.shape[1]
    return pl.pallas_call(
        _adaln_kernel,
        grid=(n // ADA_TN,),
        in_specs=[
            pl.BlockSpec(c_pad.shape, lambda j: (0, 0)),
            pl.BlockSpec((w.shape[0], ADA_TN), lambda j: (0, j)),
            pl.BlockSpec((1, ADA_TN), lambda j: (0, j)),
        ],
        out_specs=pl.BlockSpec((c_pad.shape[0], ADA_TN), lambda j: (0, j)),
        out_shape=jax.ShapeDtypeStruct((c_pad.shape[0], n), F32),
        compiler_params=_cparams(("arbitrary",)),
        name="adaln_mod",
    )(c_pad, w, b)


INPROJ_TM = 1024
INPROJ_TN = 512
NORM_ROWS = 256


def _norm_modulate(x, g, scale, shift):
    ms = jnp.mean(x * x, axis=-1, keepdims=True)
    y = (x * lax.rsqrt(ms + EPS)) * g
    return y * (1.0 + scale) + shift


def _inproj_kernel(x_ref, g_ref, sc_ref, sh_ref, w_ref, wt_ref, wtt_ref, o_ref, t_ref, tt_ref, hn_ref):
    j = pl.program_id(1)

    @pl.when(j == 0)
    def _():
        g = g_ref[...]
        sc = sc_ref[0]
        sh = sh_ref[0]
        for r in range(0, INPROJ_TM, NORM_ROWS):
            hn = _norm_modulate(x_ref[r:r + NORM_ROWS, :], g, sc, sh)
            hn_ref[r:r + NORM_ROWS, :] = hn.astype(BF16)
        hnb = hn_ref[...]
        t_ref[...] = _dot(hnb, wt_ref[...].astype(BF16))
        tt_ref[...] = _dot(wtt_ref[...].astype(BF16), hnb, NT)

    acc = _dot(hn_ref[...], w_ref[...].astype(BF16))
    for hh in range(INPROJ_TN // HEAD_LANES):
        o_ref[0, 0, hh] = acc[:, hh * HEAD_LANES:(hh + 1) * HEAD_LANES]


def _inproj(x2, g_mix, scale1, shift1, w_in, w_tail, w_tail_t, batch, seq):
    n_tok, d = x2.shape
    tiles_per_batch = seq // INPROJ_TM
    heads_per_tile = INPROJ_TN // HEAD_LANES
    tiles_per_group = 1024 // INPROJ_TN
    grid = (n_tok // INPROJ_TM, MAIN_COLS // INPROJ_TN)
    return pl.pallas_call(
        _inproj_kernel,
        grid=grid,
        in_specs=[
            pl.BlockSpec((INPROJ_TM, d), lambda i, j: (i, 0)),
            pl.BlockSpec((1, d), lambda i, j: (0, 0)),
            pl.BlockSpec((1, 1, d), lambda i, j: (i // tiles_per_batch, 0, 0)),
            pl.BlockSpec((1, 1, d), lambda i, j: (i // tiles_per_batch, 0, 0)),
            pl.BlockSpec((d, INPROJ_TN), lambda i, j: (0, j)),
            pl.BlockSpec((d, HEAD_LANES), lambda i, j: (0, 0)),
            pl.BlockSpec((16, d), lambda i, j: (0, 0)),
        ],
        out_specs=[
            pl.BlockSpec((1, 1, heads_per_tile, INPROJ_TM, HEAD_LANES),
                         lambda i, j: (j // tiles_per_group, i // tiles_per_batch, j % tiles_per_group,
                                       i % tiles_per_batch, 0)),
            pl.BlockSpec((INPROJ_TM, HEAD_LANES), lambda i, j: (i, 0)),
            pl.BlockSpec((16, INPROJ_TM), lambda i, j: (0, i)),
        ],
        out_shape=[
            jax.ShapeDtypeStruct((N_MAIN_GROUPS, batch, 8, seq, HEAD_LANES), F32),
            jax.ShapeDtypeStruct((n_tok, HEAD_LANES), F32),
            jax.ShapeDtypeStruct((16, n_tok), F32),
        ],
        scratch_shapes=[pltpu.VMEM((INPROJ_TM, d), BF16)],
        compiler_params=_cparams(("arbitrary", "arbitrary")),
        name="norm_inproj",
    )(x2, g_mix, scale1, shift1, w_in, w_tail, w_tail_t)


ATT_T = 256


def _t5_bucket(n):
    n = jnp.maximum(n, 0)
    max_exact = N_BUCKETS // 2
    nf = jnp.maximum(n, max_exact).astype(F32)
    large = max_exact + (jnp.log(nf / max_exact) / math.log(MAX_DISTANCE / max_exact)
                         * (N_BUCKETS - max_exact)).astype(jnp.int32)
    large = jnp.minimum(large, N_BUCKETS - 1)
    return jnp.where(n < max_exact, n, large)


def _bias_tiles(rel_bias):
    assert ATT_T >= MAX_DISTANCE
    t = ATT_T
    r = jnp.arange(t, dtype=jnp.int32)[:, None]
    c = jnp.arange(t, dtype=jnp.int32)[None, :]
    table = rel_bias.astype(F32)
    rel0 = r - c
    b0 = table[_t5_bucket(rel0)].transpose(2, 0, 1)
    b0 = jnp.where(rel0 >= 0, b0, F32(NEG_BIG))
    b1 = table[_t5_bucket(rel0 + t)].transpose(2, 0, 1)
    far = table[N_BUCKETS - 1]
    return jnp.stack([b0, b1], axis=1), far


def _attn_kernel(far_ref, q_ref, k_ref, v_ref, bias_ref, lq1_ref, lk1_ref, lq2_ref, lk2_ref, gs_ref,
                 o_ref, kb_ref, vb_ref):
    h = pl.program_id(1)
    i = pl.program_id(2)
    t = ATT_T

    @pl.when(i == 0)
    def _():
        kb_ref[...] = k_ref[0, 0, 0].astype(BF16)
        vb_ref[...] = v_ref[0, 0, 0].astype(BF16)

    q = q_ref[0, 0, 0] * (DIFF_HEAD_DIM ** -0.5)
    lane = lax.broadcasted_iota(jnp.int32, q.shape, 1)
    q1 = jnp.where(lane < DIFF_HEAD_DIM, q, 0.0)
    q2 = jnp.where(lane >= DIFF_HEAD_DIM, q, 0.0)
    q12 = jnp.concatenate([q1, q2], axis=0).astype(BF16)

    def step(j, carry, bias):
        m, l, acc = carry
        kblk = kb_ref[pl.ds(j * t, t), :]
        vblk = vb_ref[pl.ds(j * t, t), :]
        s = _dot(q12, kblk, NT) + bias
        m_new = jnp.maximum(m, jnp.max(s, axis=-1, keepdims=True))
        alpha = jnp.exp(m - m_new)
        p = jnp.exp(s - m_new)
        l_new = alpha * l + jnp.sum(p, axis=-1, keepdims=True)
        acc_new = alpha * acc + _dot(p.astype(BF16), vblk)
        return m_new, l_new, acc_new

    far = far_ref[h]
    m0 = jnp.full((2 * t, 1), -jnp.inf, F32)
    l0 = jnp.zeros((2 * t, 1), F32)
    a0 = jnp.zeros((2 * t, DIFF_V_DIM), F32)
    carry = lax.fori_loop(0, jnp.maximum(i - 1, 0), lambda j, c: step(j, c, far), (m0, l0, a0))
    b1 = jnp.concatenate([bias_ref[0, 1], bias_ref[0, 1]], axis=0)
    carry = lax.cond(i >= 1, lambda c: step(jnp.maximum(i - 1, 0), c, b1), lambda c: c, carry)
    b0 = jnp.concatenate([bias_ref[0, 0], bias_ref[0, 0]], axis=0)
    m, l, acc = step(i, carry, b0)

    lam = (jnp.exp(jnp.sum(lq1_ref[...] * lk1_ref[...], axis=-1, keepdims=True))
           - jnp.exp(jnp.sum(lq2_ref[...] * lk2_ref[...], axis=-1, keepdims=True)) + LAMBDA_INIT)
    o = acc[:t] / l[:t] - lam * (acc[t:] / l[t:])
    ms = jnp.mean(o * o, axis=-1, keepdims=True)
    y = (o * lax.rsqrt(ms + EPS)) * gs_ref[...]
    o_ref[0] = (y * (1.0 - LAMBDA_INIT)).astype(o_ref.dtype)


def _diff_attention(proj_main, bias_tiles, far, lq1, lk1, lq2, lk2, g_subln, batch, seq):
    t = ATT_T
    nq = seq // t
    lam_spec = pl.BlockSpec((1, DIFF_HEAD_DIM), lambda b, h, i: (0, 0))
    return pl.pallas_call(
        _attn_kernel,
        grid=(batch, DIFF_HEADS, nq),
        in_specs=[
            pl.BlockSpec(memory_space=pltpu.SMEM),
            pl.BlockSpec((1, 1, 1, t, HEAD_LANES), lambda b, h, i: (0, b, h, i, 0)),
            pl.BlockSpec((1, 1, 1, seq, HEAD_LANES), lambda b, h, i: (1, b, h, 0, 0)),
            pl.BlockSpec((1, 1, 1, seq, HEAD_LANES), lambda b, h, i: (2, b, h, 0, 0)),
            pl.BlockSpec((1, 2, t, t), lambda b, h, i: (h, 0, 0, 0)),
            lam_spec, lam_spec, lam_spec, lam_spec,
            pl.BlockSpec((1, DIFF_V_DIM), lambda b, h, i: (0, 0)),
        ],
        out_specs=pl.BlockSpec((1, t, HEAD_LANES), lambda b, h, i: (b, i, h)),
        scratch_shapes=[pltpu.VMEM((seq, HEAD_LANES), BF16), pltpu.VMEM((seq, HEAD_LANES), BF16)],
        out_shape=jax.ShapeDtypeStruct((batch, seq, DIFF_HEADS * DIFF_V_DIM), BF16),
        compiler_params=_cparams(("arbitrary", "arbitrary", "arbitrary")),
        name="diff_attention",
    )(far, proj_main, proj_main, proj_main, bias_tiles, lq1, lk1, lq2, lk2, g_subln)


GDN_C = 256
GDN_BASE = 16


def _softplus(x):
    return jnp.maximum(x, 0.0) + jnp.log1p(jnp.exp(-jnp.abs(x)))


def _silu(x):
    return x * jax.nn.sigmoid(x)


def _mm(a, b):
    return _dot(a.astype(BF16), b.astype(BF16))


def _mm3(a, b):
    a_hi = a.astype(BF16)
    a_lo = (a - a_hi.astype(F32)).astype(BF16)
    b_hi = b.astype(BF16)
    b_lo = (b - b_hi.astype(F32)).astype(BF16)
    return _dot(a_hi, b_hi) + (_dot(a_hi, b_lo) + _dot(a_lo, b_hi))


def _causal_conv_silu(ref, r0, first, cw):
    cur = ref[0, 0, 0, pl.ds(r0, GDN_C), :]
    prev = ref[0, 0, 0, pl.ds(jnp.maximum(r0 - 8, 0), 8), :]
    prev = jnp.where(first, 0.0, prev)
    ext = jnp.concatenate([prev, cur], axis=0)
    acc = cur * cw[CONV_K - 1:CONV_K, :]
    for k in range(1, CONV_K):
        acc = acc + ext[8 - k:8 - k + GDN_C, :] * cw[CONV_K - 1 - k:CONV_K - k, :]
    return _silu(acc)


def _gdn_kernel(nq_ref, nk_ref, nv_ref, ng_ref, tail_ref, arow_ref, cwq_ref, cwk_ref, cwv_ref,
                alog_ref, dtb_ref, gn_ref, o_ref, gcr_ref):
    h = pl.program_id(1)
    c = GDN_C
    n_chunks = nq_ref.shape[3] // c
    coef = -jnp.exp(alog_ref[0])
    dtb = dtb_ref[0]

    ri = lax.broadcasted_iota(jnp.int32, (c, c), 0)
    ci = lax.broadcasted_iota(jnp.int32, (c, c), 1)
    tril = ci <= ri
    strict = ci < ri
    tril_b = _ones_where(tril)
    triu_b = _ones_where(ri <= ci)
    sel_r = lax.broadcasted_iota(jnp.int32, (HEAD_LANES, 2 * HEAD_LANES), 0)
    sel_c = lax.broadcasted_iota(jnp.int32, (HEAD_LANES, 2 * HEAD_LANES), 1)
    sel = _ones_where(sel_r == jnp.where(sel_c < HEAD_LANES, h, DN_HEADS + h))

    g_rows = coef[:, :1] * _softplus(arow_ref[0, 0] + dtb[:, :1])
    g_rows = jnp.concatenate([g_rows, jnp.zeros((16 - n_chunks, c), F32)], axis=0)
    gcr_ref[...] = _dot_exact_lhs(g_rows, triu_b)

    cwq = cwq_ref[...]
    cwk = cwk_ref[...]
    cwv = cwv_ref[...]
    gn = gn_ref[...]

    def chunk(n, state):
        r0 = pl.multiple_of(n * c, c)
        first = n == 0
        cq = _causal_conv_silu(nq_ref, r0, first, cwq)
        ck = _causal_conv_silu(nk_ref, r0, first, cwk)
        cv = _causal_conv_silu(nv_ref, r0, first, cwv)
        qn = (cq * lax.rsqrt(jnp.sum(cq * cq, axis=-1, keepdims=True) + EPS)) * (DN_HEAD_DIM ** -0.5)
        kn = ck * lax.rsqrt(jnp.sum(ck * ck, axis=-1, keepdims=True) + EPS)

        ba = _dot_exact_lhs(tail_ref[pl.ds(r0, c), :], sel)
        beta = jax.nn.sigmoid(ba[:, :HEAD_LANES])
        g_col = coef * _softplus(ba[:, HEAD_LANES:] + dtb)
        gc = _dot_exact_rhs(tril_b, g_col)
        gl = gc[c - 1:c, :]
        eg = jnp.exp(gc)
        g_row = gcr_ref[pl.ds(n, 1), :]

        kb = kn * beta
        vb = cv * beta
        kkqk = _dot(jnp.concatenate([kb, qn], axis=0).astype(BF16), kn.astype(BF16), NT)
        diff = jnp.concatenate([gc, gc], axis=1) - g_row
        decay = jnp.exp(jnp.where(tril, diff, -jnp.inf))
        lmat = jnp.where(strict, kkqk[:c] * decay, 0.0)
        attn = kkqk[c:] * decay

        rb = ri // GDN_BASE
        cb = ci // GDN_BASE
        p = jnp.where(rb == cb, lmat, 0.0)
        tinv = jnp.where(ri == ci, 1.0, 0.0) - p
        for _ in range(int(math.log2(GDN_BASE)) - 1):
            p = _mm3(p, p)
            tinv = tinv + _mm3(tinv, p)
        blk = GDN_BASE
        while blk < c:
            off = jnp.where((ri // (2 * blk) == ci // (2 * blk)) & (ri // blk != ci // blk), lmat, 0.0)
            tinv = tinv - _mm3(_mm3(tinv, off), tinv)
            blk *= 2
        x = _mm(tinv, jnp.concatenate([vb, kb * eg], axis=1))
        u = x[:, :HEAD_LANES]
        w = x[:, HEAD_LANES:]

        sb = state.astype(BF16)
        v_new = u - _dot(w.astype(BF16), sb)
        vnb = v_new.astype(BF16)
        out = _dot((qn * eg).astype(BF16), sb) + _dot(attn.astype(BF16), vnb)
        k_tail = (kn * jnp.exp(gl - gc)).astype(BF16)
        state = state * jnp.exp(gl) + _dot(k_tail, vnb, ((0,), (0,)))

        ms = jnp.mean(out * out, axis=-1, keepdims=True)
        y = (out * lax.rsqrt(ms + EPS)) * gn
        gate = ng_ref[0, 0, 0, pl.ds(r0, c), :]
        o_ref[0, pl.ds(r0, c), :] = (y * _silu(gate)).astype(o_ref.dtype)
        return state

    lax.fori_loop(0, n_chunks, chunk, jnp.zeros((DN_HEAD_DIM, DN_HEAD_DIM), F32))


def _gated_deltanet(proj_main, tail, tail_t, conv_w, a_log_b, dt_bias_b, g_gnorm, batch, seq):
    c = GDN_C
    n_chunks = seq // c
    assert n_chunks <= 16
    arow = tail_t.reshape(16, batch, n_chunks, c)
    head_spec = lambda g: pl.BlockSpec((1, 1, 1, seq, HEAD_LANES), lambda b, h: (g, b, h, 0, 0))
    cw_spec = lambda g: pl.BlockSpec((CONV_K, HEAD_LANES), lambda b, h: (0, g * DN_HEADS + h))
    vec_spec = pl.BlockSpec((1, 1, HEAD_LANES), lambda b, h: (h, 0, 0))
    return pl.pallas_call(
        _gdn_kernel,
        grid=(batch, DN_HEADS),
        in_specs=[
            head_spec(3), head_spec(4), head_spec(5), head_spec(6),
            pl.BlockSpec((seq, HEAD_LANES), lambda b, h: (b, 0)),
            pl.BlockSpec((1, 1, n_chunks, c), lambda b, h: (DN_HEADS + h, b, 0, 0)),
            cw_spec(0), cw_spec(1), cw_spec(2),
            vec_spec, vec_spec,
            pl.BlockSpec((1, DN_HEAD_DIM), lambda b, h: (0, 0)),
        ],
        out_specs=pl.BlockSpec((1, seq, HEAD_LANES), lambda b, h: (b, 0, h)),
        out_shape=jax.ShapeDtypeStruct((batch, seq, DN_HEADS * DN_HEAD_DIM), BF16),
        scratch_shapes=[pltpu.VMEM((16, c), F32)],
        compiler_params=_cparams(("arbitrary", "arbitrary")),
        name="gated_deltanet",
    )(proj_main, proj_main, proj_main, proj_main, tail, arow, conv_w, conv_w, conv_w,
      a_log_b, dt_bias_b, g_gnorm)


OUT_TM = 512
OUT_TK = 512
EPI_ROWS = 256


def _dot_f32(a, b_parts):
    a_hi, a_mid, a_lo = _split3(a)
    b_hi, b_mid, b_lo = b_parts
    acc = _dot(a_hi, b_hi)
    acc += _dot(a_hi, b_mid) + _dot(a_mid, b_hi)
    acc += _dot(a_hi, b_lo) + _dot(a_mid, b_mid) + _dot(a_lo, b_hi)
    return acc


def _outproj_kernel(od_ref, on_ref, w_ref, x_ref, g1_ref, gf_ref, sc_ref, sh_ref, wr_ref, br_ref,
                    x1_ref, hn2_ref, idx_ref, gate_ref, rank_ref, cnt_ref, acc_ref, mh_ref, carry_ref):
    i = pl.program_id(0)
    k = pl.program_id(1)
    nk = pl.num_programs(1)
    tm = OUT_TM

    @pl.when((i == 0) & (k == 0))
    def _():
        carry_ref[...] = jnp.zeros_like(carry_ref)

    lhs = jnp.where(k < nk // 2, od_ref[...], on_ref[...])
    part = _dot(lhs, w_ref[...].astype(BF16))

    @pl.when(k == 0)
    def _():
        acc_ref[...] = part

    @pl.when(k > 0)
    def _():
        acc_ref[...] += part

    @pl.when(k == nk - 1)
    def _():
        g1 = g1_ref[0]
        gf = gf_ref[...]
        sc = sc_ref[0]
        sh = sh_ref[0]
        wr_parts = _split3(wr_ref[...])
        br = br_ref[...]
        lane = lax.broadcasted_iota(jnp.int32, (EPI_ROWS, HEAD_LANES), 1)
        for r in range(0, tm, EPI_ROWS):
            rows = slice(r, r + EPI_ROWS)
            x1 = x_ref[rows, :] + g1 * acc_ref[rows, :]
            x1_ref[rows, :] = x1
            hn2 = _norm_modulate(x1, gf, sc, sh)
            hn2_ref[rows, :] = hn2
            logits = _dot_f32(hn2, wr_parts) + br
            l = jnp.where(lane < N_EXPERTS, logits, -jnp.inf)
            vals, idxs = [], []
            for _ in range(TOP_K):
                m = jnp.max(l, axis=-1, keepdims=True)
                ix = jnp.min(jnp.where(l == m, lane, HEAD_LANES), axis=-1, keepdims=True)
                vals.append(m)
                idxs.append(ix)
                l = jnp.where(lane == ix, -jnp.inf, l)
            es = [jnp.exp(v - vals[0]) for v in vals]
            den = (es[0] + es[1]) + (es[2] + es[3])
            idx_o = jnp.zeros((EPI_ROWS, HEAD_LANES), jnp.int32)
            gate_o = jnp.zeros((EPI_ROWS, HEAD_LANES), F32)
            hot = jnp.zeros((EPI_ROWS, HEAD_LANES), F32)
            for kk in range(TOP_K):
                idx_o = jnp.where(lane == kk, idxs[kk], idx_o)
                gate_o = jnp.where(lane == kk, es[kk] / den, gate_o)
                hot = jnp.where(lane == idxs[kk], 1.0, hot)
            idx_ref[rows, :] = idx_o
            gate_ref[rows, :] = gate_o
            mh_ref[rows, :] = hot.astype(BF16)

        ri = lax.broadcasted_iota(jnp.int32, (tm, tm), 0)
        ci = lax.broadcasted_iota(jnp.int32, (tm, tm), 1)
        mh = mh_ref[...]
        carry = carry_ref[0:1, :]
        cum = _dot(_ones_where(ci < ri), mh) + carry
        lane_t = lax.broadcasted_iota(jnp.int32, (tm, HEAD_LANES), 1)
        idx_t = idx_ref[...]
        rank_o = jnp.zeros((tm, HEAD_LANES), F32)
        for kk in range(TOP_K):
            ik = idx_t[:, kk:kk + 1]
            rk = jnp.sum(jnp.where(lane_t == ik, cum, 0.0), axis=-1, keepdims=True)
            rank_o = jnp.where(lane_t == kk, rk, rank_o)
        rank_ref[...] = rank_o.astype(jnp.int32)
        total = carry + jnp.sum(mh.astype(F32), axis=0, keepdims=True)
        carry_ref[...] = jnp.broadcast_to(total, carry_ref.shape)
        cnt_ref[...] = jnp.broadcast_to(total, cnt_ref.shape).astype(jnp.int32)


def _outproj_router(o_diff, o_dn, w_out, x2, gate1, g_ffn, scale2, shift2, w_router_pad, b_router_pad, seq):
    n_tok, d = x2.shape
    tm, tk = OUT_TM, OUT_TK
    tiles_per_batch = seq // tm
    nk = d // tk
    half = nk // 2
    vec3 = pl.BlockSpec((1, 1, d), lambda i, k: (i // tiles_per_batch, 0, 0))
    row_out = lambda dt: jax.ShapeDtypeStruct((n_tok, HEAD_LANES), dt)
    return pl.pallas_call(
        _outproj_kernel,
        grid=(n_tok // tm, nk),
        in_specs=[
            pl.BlockSpec((tm, tk), lambda i, k: (i, jnp.minimum(k, half - 1))),
            pl.BlockSpec((tm, tk), lambda i, k: (i, jnp.maximum(k - half, 0))),
            pl.BlockSpec((tk, d), lambda i, k: (k, 0)),
            pl.BlockSpec((tm, d), lambda i, k: (i, 0)),
            vec3,
            pl.BlockSpec((1, d), lambda i, k: (0, 0)),
            vec3, vec3,
            pl.BlockSpec((d, HEAD_LANES), lambda i, k: (0, 0)),
            pl.BlockSpec((1, HEAD_LANES), lambda i, k: (0, 0)),
        ],
        out_specs=[
            pl.BlockSpec((tm, d), lambda i, k: (i, 0)),
            pl.BlockSpec((tm, d), lambda i, k: (i, 0)),
            pl.BlockSpec((tm, HEAD_LANES), lambda i, k: (i, 0)),
            pl.BlockSpec((tm, HEAD_LANES), lambda i, k: (i, 0)),
            pl.BlockSpec((tm, HEAD_LANES), lambda i, k: (i, 0)),
            pl.BlockSpec((8, HEAD_LANES), lambda i, k: (0, 0)),
        ],
        out_shape=[
            jax.ShapeDtypeStruct((n_tok, d), F32),
            jax.ShapeDtypeStruct((n_tok, d), F32),
            row_out(jnp.int32), row_out(F32), row_out(jnp.int32),
            jax.ShapeDtypeStruct((8, HEAD_LANES), jnp.int32),
        ],
        scratch_shapes=[pltpu.VMEM((tm, d), F32), pltpu.VMEM((tm, HEAD_LANES), BF16),
                        pltpu.VMEM((8, HEAD_LANES), F32)],
        compiler_params=_cparams(("arbitrary", "arbitrary")),
        name="outproj_router",
    )(o_diff, o_dn, w_out, x2, gate1, g_ffn, scale2, shift2, w_router_pad, b_router_pad)


MOE_TM = 1024
MOE_TF = 256
MOE_SUB = 256


def _row_copy(src_hbm, row, dst_vmem, slot, sem):
    return pltpu.make_async_copy(src_hbm.at[pl.ds(row, 1)], dst_vmem.at[pl.ds(slot, 1)], sem)


def _moe_kernel(sbe_ref, sbr_ref, nused_ref, tok_ref, hn2_hbm, wg_ref, wu_ref, wd_ref, bg_ref, bu_ref, bd_ref,
                o_ref, stage_ref, xb_ref, sem):
    s = pl.program_id(0)
    f = pl.program_id(1)
    nv = sbr_ref[s]

    @pl.when((s == 0) & (f == 0))
    def _():
        stage_ref[...] = jnp.zeros_like(stage_ref)

    @pl.when((f == 0) & (nv > 0))
    def _():
        def issue(r, carry):
            _row_copy(hn2_hbm, tok_ref[0, 0, r], stage_ref, r, sem).start()
            return carry

        lax.fori_loop(0, nv, issue, 0)

        def wait(r, carry):
            _row_copy(hn2_hbm, 0, stage_ref, 0, sem).wait()
            return carry

        lax.fori_loop(0, nv, wait, 0)
        for r in range(0, MOE_TM, MOE_SUB):
            xb_ref[r:r + MOE_SUB, :] = stage_ref[r:r + MOE_SUB, :].astype(BF16)

    wg = wg_ref[0].astype(BF16)
    wu = wu_ref[0].astype(BF16)
    wd = wd_ref[0].astype(BF16)
    bg = bg_ref[0]
    bu = bu_ref[0]

    for q in range(MOE_TM // MOE_SUB):
        rows = slice(q * MOE_SUB, (q + 1) * MOE_SUB)

        @pl.when(q * MOE_SUB < nv)
        def _():
            xq = xb_ref[rows, :]
            gate = jnp.minimum(_dot(xq, wg) + bg, SWIGLU_LIMIT)
            up = jnp.clip(_dot(xq, wu) + bu, -SWIGLU_LIMIT, SWIGLU_LIMIT)
            glu = gate * jax.nn.sigmoid(gate * SWIGLU_ALPHA)
            part = _dot(((up + 1.0) * glu).astype(BF16), wd)

            @pl.when(f == 0)
            def _():
                o_ref[rows, :] = part + bd_ref[0]

            @pl.when(f > 0)
            def _():
                o_ref[rows, :] += part

        @pl.when((q * MOE_SUB >= nv) & (f == 0))
        def _():
            o_ref[rows, :] = jnp.zeros((MOE_SUB, o_ref.shape[1]), F32)


def _moe_experts(sb_expert, sb_rows, n_used, tok_sorted, hn2, w_gate_up, b_gate_up, w_down, b_down):
    n_sb = sb_expert.shape[0]
    d = hn2.shape[1]
    f_dim = w_down.shape[1]
    nf = f_dim // MOE_TF

    def f_eff(s, f, nused):
        return jnp.where(s < nused[0], f, nf - 1)

    grid_spec = pltpu.PrefetchScalarGridSpec(
        num_scalar_prefetch=3,
        grid=(n_sb, nf),
        in_specs=[
            pl.BlockSpec((1, 1, MOE_TM), lambda s, f, sbe, sbr, nu: (s, 0, 0), memory_space=pltpu.SMEM),
            pl.BlockSpec(memory_space=pl.ANY),
            pl.BlockSpec((1, d, MOE_TF), lambda s, f, sbe, sbr, nu: (sbe[s], 0, f_eff(s, f, nu))),
            pl.BlockSpec((1, d, MOE_TF), lambda s, f, sbe, sbr, nu: (sbe[s], 0, nf + f_eff(s, f, nu))),
            pl.BlockSpec((1, MOE_TF, d), lambda s, f, sbe, sbr, nu: (sbe[s], f_eff(s, f, nu), 0)),
            pl.BlockSpec((1, 1, MOE_TF), lambda s, f, sbe, sbr, nu: (sbe[s], 0, f_eff(s, f, nu))),
            pl.BlockSpec((1, 1, MOE_TF), lambda s, f, sbe, sbr, nu: (sbe[s], 0, nf + f_eff(s, f, nu))),
            pl.BlockSpec((1, 1, d), lambda s, f, sbe, sbr, nu: (sbe[s], 0, 0)),
        ],
        out_specs=pl.BlockSpec((MOE_TM, d), lambda s, f, sbe, sbr, nu: (s, 0)),
        scratch_shapes=[pltpu.VMEM((MOE_TM, d), F32), pltpu.VMEM((MOE_TM, d), BF16),
                        pltpu.SemaphoreType.DMA(())],
    )
    n_exp = w_down.shape[0]
    return pl.pallas_call(
        _moe_kernel,
        grid_spec=grid_spec,
        out_shape=jax.ShapeDtypeStruct((n_sb * MOE_TM, d), F32),
        compiler_params=_cparams(("arbitrary", "arbitrary")),
        name="moe_experts",
    )(sb_expert, sb_rows, n_used, tok_sorted, hn2, w_gate_up, w_gate_up, w_down,
      b_gate_up.reshape(n_exp, 1, 2 * f_dim), b_gate_up.reshape(n_exp, 1, 2 * f_dim),
      b_down.reshape(n_exp, 1, d))


CMB_TM = 256


def _combine_kernel(dest_ref, ys_hbm, x1_ref, gate_ref, g2_ref, gfin_ref, o_ref, buf_ref, sem):
    tm = CMB_TM

    def issue(r, carry):
        for kk in range(TOP_K):
            _row_copy(ys_hbm, dest_ref[0, 0, r * TOP_K + kk], buf_ref.at[kk], r, sem).start()
        return carry

    lax.fori_loop(0, tm, issue, 0)

    def wait(r, carry):
        for kk in range(TOP_K):
            _row_copy(ys_hbm, 0, buf_ref.at[kk], 0, sem).wait()
        return carry

    lax.fori_loop(0, tm, wait, 0)

    gates = gate_ref[...]
    y = gates[:, 0:1] * buf_ref[0]
    for kk in range(1, TOP_K):
        y = y + gates[:, kk:kk + 1] * buf_ref[kk]
    x2 = x1_ref[...] + g2_ref[0] * y
    ms = jnp.mean(x2 * x2, axis=-1, keepdims=True)
    o_ref[...] = (x2 * lax.rsqrt(ms + EPS)) * gfin_ref[...]


def _combine(dest_tiles, ys, x1, gates, gate2, g_final, seq):
    n_tok, d = x1.shape
    tm = CMB_TM
    tiles_per_batch = seq // tm
    return pl.pallas_call(
        _combine_kernel,
        grid=(n_tok // tm,),
        in_specs=[
            pl.BlockSpec((1, 1, tm * TOP_K), lambda i: (i, 0, 0), memory_space=pltpu.SMEM),
            pl.BlockSpec(memory_space=pl.ANY),
            pl.BlockSpec((tm, d), lambda i: (i, 0)),
            pl.BlockSpec((tm, HEAD_LANES), lambda i: (i, 0)),
            pl.BlockSpec((1, 1, d), lambda i: (i // tiles_per_batch, 0, 0)),
            pl.BlockSpec((1, d), lambda i: (0, 0)),
        ],
        out_specs=pl.BlockSpec((tm, d), lambda i: (i, 0)),
        out_shape=jax.ShapeDtypeStruct((n_tok, d), F32),
        scratch_shapes=[pltpu.VMEM((TOP_K, tm, d), F32), pltpu.SemaphoreType.DMA(())],
        compiler_params=_cparams(("arbitrary",)),
        name="moe_combine_norm",
    )(dest_tiles, ys, x1, gates, gate2, g_final)


def kernel(x, c, w_ada, b_ada, g_norm_mix, g_norm_ffn, g_norm_final, w_in, conv_w, rel_bias,
           lambda_q1, lambda_k1, lambda_q2, lambda_k2, g_subln, a_log, dt_bias, g_gnorm, w_out,
           w_router, b_router, w_gate_up, b_gate_up, w_down, b_down):
    batch, seq, d = x.shape
    n_tok = batch * seq
    depth = w_ada.shape[0]
    assert depth == 1 and d == D_MODEL
    x2 = x.reshape(n_tok, d)
    bias_tiles, far = _bias_tiles(rel_bias)
    for l in range(depth):
        c_pad = jnp.zeros((16, d), F32).at[:batch].set(c)
        mod = _adaln(c_pad, w_ada[l], b_ada[l][None, :])[:batch]
        shift1, scale1, gate1, shift2, scale2, gate2 = [m.reshape(batch, 1, d) for m in jnp.split(mod, 6, axis=-1)]

        w_tail = jnp.pad(w_in[l][:, MAIN_COLS:], ((0, 0), (0, HEAD_LANES - 2 * DN_HEADS)))
        w_tail_t = w_in[l][:, MAIN_COLS:].T
        proj_main, tail, tail_t = _inproj(x2, g_norm_mix[l][None, :], scale1, shift1, w_in[l], w_tail, w_tail_t,
                                          batch, seq)
        o_diff = _diff_attention(proj_main, bias_tiles, far, lambda_q1[l][None, :], lambda_k1[l][None, :],
                                 lambda_q2[l][None, :], lambda_k2[l][None, :], g_subln[l][None, :], batch, seq)
        a_log_b = jnp.broadcast_to(a_log[l][:, None, None], (DN_HEADS, 1, HEAD_LANES))
        dt_bias_b = jnp.broadcast_to(dt_bias[l][:, None, None], (DN_HEADS, 1, HEAD_LANES))
        o_dn = _gated_deltanet(proj_main, tail, tail_t, conv_w[l], a_log_b, dt_bias_b, g_gnorm[l][None, :],
                               batch, seq)

        w_router_pad = jnp.pad(w_router[l], ((0, 0), (0, HEAD_LANES - N_EXPERTS)))
        b_router_pad = jnp.pad(b_router[l], (0, HEAD_LANES - N_EXPERTS))[None, :]
        x1, hn2, top_idx, gates, rank, counts = _outproj_router(
            o_diff.reshape(n_tok, -1), o_dn.reshape(n_tok, -1), w_out[l], x2, gate1, g_norm_ffn[l][None, :],
            scale2, shift2, w_router_pad, b_router_pad, seq)

        n_assign = n_tok * TOP_K
        n_sb = n_assign // MOE_TM + N_EXPERTS
        cnt = counts[0, :N_EXPERTS]
        sb_per_e = (cnt + MOE_TM - 1) // MOE_TM
        sb_end = jnp.cumsum(sb_per_e)
        sb_start = sb_end - sb_per_e
        n_used = sb_end[-1]
        s_ids = jnp.arange(n_sb, dtype=jnp.int32)
        e_of_s = jnp.minimum(jnp.searchsorted(sb_end, jnp.minimum(s_ids, n_used - 1), side='right'),
                             N_EXPERTS - 1).astype(jnp.int32)
        rows_of_s = jnp.clip(cnt[e_of_s] - (s_ids - sb_start[e_of_s]) * MOE_TM, 0, MOE_TM)
        sb_rows = jnp.where(s_ids < n_used, rows_of_s, 0).astype(jnp.int32)
        idx4 = top_idx[:, :TOP_K]
        dest = (sb_start[idx4] * MOE_TM + rank[:, :TOP_K]).astype(jnp.int32)
        tok_sorted = jnp.zeros((n_sb * MOE_TM,), jnp.int32).at[dest.reshape(-1)].set(
            jnp.arange(n_assign, dtype=jnp.int32) // TOP_K)

        ys = _moe_experts(e_of_s, sb_rows, n_used.reshape(1).astype(jnp.int32), tok_sorted.reshape(n_sb, 1, MOE_TM),
                          hn2, w_gate_up[l], b_gate_up[l], w_down[l], b_down[l])
        x2 = _combine(dest.reshape(n_tok // CMB_TM, 1, CMB_TM * TOP_K), ys, x1, gates, gate2,
                      g_norm_final[None, :], seq)
    return x2.reshape(batch, seq, d)
```

```python
import functools
import math

import jax
import jax.numpy as jnp
import numpy as np
from jax import lax
from jax.experimental import pallas as pl
from jax.experimental.pallas import tpu as pltpu

F32 = jnp.float32
BF16 = jnp.bfloat16

D_MODEL = 2048
DIFF_HEADS = 8
DIFF_HEAD_DIM = 64
DIFF_V_DIM = 128
DN_HEADS = 8
DN_HEAD_DIM = 128
HEAD_LANES = 128
CONV_K = 4
N_BUCKETS = 32
MAX_DISTANCE = 128
N_EXPERTS = 32
TOP_K = 4
D_EXPERT = 2048
SWIGLU_LIMIT = 7.0
SWIGLU_ALPHA = 1.702
EPS = 1e-6
LAMBDA_INIT = 0.8 - 0.6 * math.exp(-0.3 * 0)
N_MAIN_GROUPS = 7
MAIN_COLS = N_MAIN_GROUPS * 1024

V7X_VMEM_BYTES = 64 * 1024 * 1024
VMEM_LIMIT = 56 * 1024 * 1024

NEG_BIG = -1e30


def _cparams(sem):
    return pltpu.CompilerParams(dimension_semantics=sem, vmem_limit_bytes=VMEM_LIMIT)


def _split3(a):
    hi = a.astype(BF16)
    r1 = a - hi.astype(F32)
    mid = r1.astype(BF16)
    lo = (r1 - mid.astype(F32)).astype(BF16)
    return hi, mid, lo


def _dot(a, b, dims=None):
    if dims is None:
        return jnp.dot(a, b, preferred_element_type=F32)
    return lax.dot_general(a, b, (dims, ((), ())), preferred_element_type=F32)


def _dot_exact_lhs(a, b_bf16, dims=None):
    hi, mid, lo = _split3(a)
    return (_dot(hi, b_bf16, dims) + _dot(mid, b_bf16, dims)) + _dot(lo, b_bf16, dims)


def _dot_exact_rhs(a_bf16, b, dims=None):
    hi, mid, lo = _split3(b)
    return (_dot(a_bf16, hi, dims) + _dot(a_bf16, mid, dims)) + _dot(a_bf16, lo, dims)


NT = ((1,), (1,))


def _ones_where(mask):
    return jnp.where(mask, 1.0, 0.0).astype(BF16)


ADA_TN = 1024


def _adaln_kernel(c_ref, w_ref, b_ref, o_ref):
    c = c_ref[...]
    sc = c * jax.nn.sigmoid(c)
    a_hi, a_mid, a_lo = _split3(sc)
    w_hi, w_mid, w_lo = _split3(w_ref[...])
    acc = _dot(a_hi, w_hi)
    acc += _dot(a_hi, w_mid) + _dot(a_mid, w_hi)
    acc += _dot(a_hi, w_lo) + _dot(a_mid, w_mid) + _dot(a_lo, w_hi)
    o_ref[...] = acc + b_ref[...]


def _adaln(c_pad, w, b):
    n = w.shape[1]
    return pl.pallas_call(
        _adaln_kernel,
        grid=(n // ADA_TN,),
        in_specs=[
            pl.BlockSpec(c_pad.shape, lambda j: (0, 0)),
            pl.BlockSpec((w.shape[0], ADA_TN), lambda j: (0, j)),
            pl.BlockSpec((1, ADA_TN), lambda j: (0, j)),
        ],
        out_specs=pl.BlockSpec((c_pad.shape[0], ADA_TN), lambda j: (0, j)),
        out_shape=jax.ShapeDtypeStruct((c_pad.shape[0], n), F32),
        compiler_params=_cparams(("arbitrary",)),
        name="adaln_mod",
    )(c_pad, w, b)


INPROJ_TM = 1024
INPROJ_TN = 512
NORM_ROWS = 256


def _norm_modulate(x, g, scale, shift):
    ms = jnp.mean(x * x, axis=-1, keepdims=True)
    y = (x * lax.rsqrt(ms + EPS)) * g
    return y * (1.0 + scale) + shift


def _inproj_kernel(x_ref, g_ref, sc_ref, sh_ref, w_ref, wt_ref, wtt_ref, o_ref, t_ref, tt_ref, hn_ref):
    j = pl.program_id(1)

    @pl.when(j == 0)
    def _():
        g = g_ref[...]
        sc = sc_ref[0]
        sh = sh_ref[0]
        for r in range(0, INPROJ_TM, NORM_ROWS):
            hn = _norm_modulate(x_ref[r:r + NORM_ROWS, :], g, sc, sh)
            hn_ref[r:r + NORM_ROWS, :] = hn.astype(BF16)
        hnb = hn_ref[...]
        t_ref[...] = _dot(hnb, wt_ref[...].astype(BF16))
        tt_ref[...] = _dot(wtt_ref[...].astype(BF16), hnb, NT)

    acc = _dot(hn_ref[...], w_ref[...].astype(BF16))
    for hh in range(INPROJ_TN // HEAD_LANES):
        o_ref[0, 0, hh] = acc[:, hh * HEAD_LANES:(hh + 1) * HEAD_LANES]


def _inproj(x2, g_mix, scale1, shift1, w_in, w_tail, w_tail_t, batch, seq):
    n_tok, d = x2.shape
    tiles_per_batch = seq // INPROJ_TM
    heads_per_tile = INPROJ_TN // HEAD_LANES
    tiles_per_group = 1024 // INPROJ_TN
    grid = (n_tok // INPROJ_TM, MAIN_COLS // INPROJ_TN)
    return pl.pallas_call(
        _inproj_kernel,
        grid=grid,
        in_specs=[
            pl.BlockSpec((INPROJ_TM, d), lambda i, j: (i, 0)),
            pl.BlockSpec((1, d), lambda i, j: (0, 0)),
            pl.BlockSpec((1, 1, d), lambda i, j: (i // tiles_per_batch, 0, 0)),
            pl.BlockSpec((1, 1, d), lambda i, j: (i // tiles_per_batch, 0, 0)),
            pl.BlockSpec((d, INPROJ_TN), lambda i, j: (0, j)),
            pl.BlockSpec((d, HEAD_LANES), lambda i, j: (0, 0)),
            pl.BlockSpec((16, d), lambda i, j: (0, 0)),
        ],
        out_specs=[
            pl.BlockSpec((1, 1, heads_per_tile, INPROJ_TM, HEAD_LANES),
                         lambda i, j: (j // tiles_per_group, i // tiles_per_batch, j % tiles_per_group,
                                       i % tiles_per_batch, 0)),
            pl.BlockSpec((INPROJ_TM, HEAD_LANES), lambda i, j: (i, 0)),
            pl.BlockSpec((16, INPROJ_TM), lambda i, j: (0, i)),
        ],
        out_shape=[
            jax.ShapeDtypeStruct((N_MAIN_GROUPS, batch, 8, seq, HEAD_LANES), F32),
            jax.ShapeDtypeStruct((n_tok, HEAD_LANES), F32),
            jax.ShapeDtypeStruct((16, n_tok), F32),
        ],
        scratch_shapes=[pltpu.VMEM((INPROJ_TM, d), BF16)],
        compiler_params=_cparams(("arbitrary", "arbitrary")),
        name="norm_inproj",
    )(x2, g_mix, scale1, shift1, w_in, w_tail, w_tail_t)


ATT_T = 256


def _t5_bucket(n):
    n = jnp.maximum(n, 0)
    max_exact = N_BUCKETS // 2
    nf = jnp.maximum(n, max_exact).astype(F32)
    large = max_exact + (jnp.log(nf / max_exact) / math.log(MAX_DISTANCE / max_exact)
                         * (N_BUCKETS - max_exact)).astype(jnp.int32)
    large = jnp.minimum(large, N_BUCKETS - 1)
    return jnp.where(n < max_exact, n, large)


def _bias_tiles(rel_bias):
    assert ATT_T >= MAX_DISTANCE
    t = ATT_T
    table = rel_bias.astype(F32)
    n_heads = table.shape[1]
    tiles = []
    for d in range(2):
        dist = (d * t + t) - jnp.arange(2 * t, dtype=jnp.int32)
        w = jnp.where(dist[:, None] >= 0, table[_t5_bucket(dist)], F32(NEG_BIG)).T
        w_ext = jnp.concatenate([w, jnp.zeros((n_heads, 1), F32)], axis=1)
        skew = jnp.tile(w_ext, (1, t))[:, :t * 2 * t].reshape(n_heads, t, 2 * t)
        tiles.append(skew[:, :, t:])
    far = table[N_BUCKETS - 1]
    return jnp.stack(tiles, axis=1), far


def _attn_kernel(far_ref, q_ref, k_ref, v_ref, bias_ref, lq1_ref, lk1_ref, lq2_ref, lk2_ref, gs_ref,
                 o_ref, kb_ref, vb_ref):
    h = pl.program_id(1)
    i = pl.program_id(2)
    t = ATT_T

    @pl.when(i == 0)
    def _():
        kb_ref[...] = k_ref[0, 0, 0].astype(BF16)
        vb_ref[...] = v_ref[0, 0, 0].astype(BF16)

    q = q_ref[0, 0, 0] * (DIFF_HEAD_DIM ** -0.5)
    lane = lax.broadcasted_iota(jnp.int32, q.shape, 1)
    q1 = jnp.where(lane < DIFF_HEAD_DIM, q, 0.0)
    q2 = jnp.where(lane >= DIFF_HEAD_DIM, q, 0.0)
    q12 = jnp.concatenate([q1, q2], axis=0).astype(BF16)

    def step(j, carry, bias):
        m, l, acc = carry
        kblk = kb_ref[pl.ds(j * t, t), :]
        vblk = vb_ref[pl.ds(j * t, t), :]
        s = _dot(q12, kblk, NT) + bias
        m_new = jnp.maximum(m, jnp.max(s, axis=-1, keepdims=True))
        alpha = jnp.exp(m - m_new)
        p = jnp.exp(s - m_new)
        l_new = alpha * l + jnp.sum(p, axis=-1, keepdims=True)
        acc_new = alpha * acc + _dot(p.astype(BF16), vblk)
        return m_new, l_new, acc_new

    far = far_ref[h]
    m0 = jnp.full((2 * t, 1), -jnp.inf, F32)
    l0 = jnp.zeros((2 * t, 1), F32)
    a0 = jnp.zeros((2 * t, DIFF_V_DIM), F32)
    carry = lax.fori_loop(0, jnp.maximum(i - 1, 0), lambda j, c: step(j, c, far), (m0, l0, a0))
    b1 = jnp.concatenate([bias_ref[0, 1], bias_ref[0, 1]], axis=0)
    carry = lax.cond(i >= 1, lambda c: step(jnp.maximum(i - 1, 0), c, b1), lambda c: c, carry)
    b0 = jnp.concatenate([bias_ref[0, 0], bias_ref[0, 0]], axis=0)
    m, l, acc = step(i, carry, b0)

    lam = (jnp.exp(jnp.sum(lq1_ref[...] * lk1_ref[...], axis=-1, keepdims=True))
           - jnp.exp(jnp.sum(lq2_ref[...] * lk2_ref[...], axis=-1, keepdims=True)) + LAMBDA_INIT)
    o = acc[:t] / l[:t] - lam * (acc[t:] / l[t:])
    ms = jnp.mean(o * o, axis=-1, keepdims=True)
    y = (o * lax.rsqrt(ms + EPS)) * gs_ref[...]
    o_ref[0] = (y * (1.0 - LAMBDA_INIT)).astype(o_ref.dtype)


def _diff_attention(proj_main, bias_tiles, far, lq1, lk1, lq2, lk2, g_subln, batch, seq):
    t = ATT_T
    nq = seq // t
    lam_spec = pl.BlockSpec((1, DIFF_HEAD_DIM), lambda b, h, i: (0, 0))
    return pl.pallas_call(
        _attn_kernel,
        grid=(batch, DIFF_HEADS, nq),
        in_specs=[
            pl.BlockSpec(memory_space=pltpu.SMEM),
            pl.BlockSpec((1, 1, 1, t, HEAD_LANES), lambda b, h, i: (0, b, h, i, 0)),
            pl.BlockSpec((1, 1, 1, seq, HEAD_LANES), lambda b, h, i: (1, b, h, 0, 0)),
            pl.BlockSpec((1, 1, 1, seq, HEAD_LANES), lambda b, h, i: (2, b, h, 0, 0)),
            pl.BlockSpec((1, 2, t, t), lambda b, h, i: (h, 0, 0, 0)),
            lam_spec, lam_spec, lam_spec, lam_spec,
            pl.BlockSpec((1, DIFF_V_DIM), lambda b, h, i: (0, 0)),
        ],
        out_specs=pl.BlockSpec((1, t, HEAD_LANES), lambda b, h, i: (b, i, h)),
        scratch_shapes=[pltpu.VMEM((seq, HEAD_LANES), BF16), pltpu.VMEM((seq, HEAD_LANES), BF16)],
        out_shape=jax.ShapeDtypeStruct((batch, seq, DIFF_HEADS * DIFF_V_DIM), BF16),
        compiler_params=_cparams(("arbitrary", "arbitrary", "arbitrary")),
        name="diff_attention",
    )(far, proj_main, proj_main, proj_main, bias_tiles, lq1, lk1, lq2, lk2, g_subln)


GDN_C = 256
GDN_BASE = 16


def _softplus(x):
    return jnp.maximum(x, 0.0) + jnp.log1p(jnp.exp(-jnp.abs(x)))


def _silu(x):
    return x * jax.nn.sigmoid(x)


def _mm(a, b):
    return _dot(a.astype(BF16), b.astype(BF16))


def _causal_conv_silu(ref, r0, first, cw):
    cur = ref[0, 0, 0, pl.ds(r0, GDN_C), :]
    prev = ref[0, 0, 0, pl.ds(jnp.maximum(r0 - 8, 0), 8), :]
    prev = jnp.where(first, 0.0, prev)
    ext = jnp.concatenate([prev, cur], axis=0)
    acc = cur * cw[CONV_K - 1:CONV_K, :]
    for k in range(1, CONV_K):
        acc = acc + ext[8 - k:8 - k + GDN_C, :] * cw[CONV_K - 1 - k:CONV_K - k, :]
    return _silu(acc)


def _gdn_kernel(nq_ref, nk_ref, nv_ref, ng_ref, tail_ref, arow_ref, cwq_ref, cwk_ref, cwv_ref,
                alog_ref, dtb_ref, gn_ref, o_ref, gcr_ref):
    h = pl.program_id(1)
    c = GDN_C
    n_chunks = nq_ref.shape[3] // c
    coef = -jnp.exp(alog_ref[0])
    dtb = dtb_ref[0]

    ri = lax.broadcasted_iota(jnp.int32, (c, c), 0)
    ci = lax.broadcasted_iota(jnp.int32, (c, c), 1)
    tril = ci <= ri
    strict = ci < ri
    tril_b = _ones_where(tril)
    triu_b = _ones_where(ri <= ci)
    sel_r = lax.broadcasted_iota(jnp.int32, (HEAD_LANES, 2 * HEAD_LANES), 0)
    sel_c = lax.broadcasted_iota(jnp.int32, (HEAD_LANES, 2 * HEAD_LANES), 1)
    sel = _ones_where(sel_r == jnp.where(sel_c < HEAD_LANES, h, DN_HEADS + h))

    g_rows = coef[:, :1] * _softplus(arow_ref[0, 0] + dtb[:, :1])
    g_rows = jnp.concatenate([g_rows, jnp.zeros((16 - n_chunks, c), F32)], axis=0)
    gcr_ref[...] = _dot_exact_lhs(g_rows, triu_b)

    cwq = cwq_ref[...]
    cwk = cwk_ref[...]
    cwv = cwv_ref[...]
    gn = gn_ref[...]

    def chunk(n, state):
        r0 = pl.multiple_of(n * c, c)
        first = n == 0
        cq = _causal_conv_silu(nq_ref, r0, first, cwq)
        ck = _causal_conv_silu(nk_ref, r0, first, cwk)
        cv = _causal_conv_silu(nv_ref, r0, first, cwv)
        qn = (cq * lax.rsqrt(jnp.sum(cq * cq, axis=-1, keepdims=True) + EPS)) * (DN_HEAD_DIM ** -0.5)
        kn = ck * lax.rsqrt(jnp.sum(ck * ck, axis=-1, keepdims=True) + EPS)

        ba = _dot_exact_lhs(tail_ref[pl.ds(r0, c), :], sel)
        beta = jax.nn.sigmoid(ba[:, :HEAD_LANES])
        g_col = coef * _softplus(ba[:, HEAD_LANES:] + dtb)
        gc = _dot_exact_rhs(tril_b, g_col)
        gl = gc[c - 1:c, :]
        eg = jnp.exp(gc)
        g_row = gcr_ref[pl.ds(n, 1), :]

        kb = kn * beta
        vb = cv * beta
        kkqk = _dot(jnp.concatenate([kb, qn], axis=0).astype(BF16), kn.astype(BF16), NT)
        diff = jnp.concatenate([gc, gc], axis=1) - g_row
        decay = jnp.exp(jnp.where(tril, diff, -jnp.inf))
        lmat = jnp.where(strict, kkqk[:c] * decay, 0.0)
        attn = kkqk[c:] * decay

        rb = ri // GDN_BASE
        cb = ci // GDN_BASE
        p = jnp.where(rb == cb, lmat, 0.0)
        tinv = jnp.where(ri == ci, 1.0, 0.0) - p
        for _ in range(int(math.log2(GDN_BASE)) - 1):
            p = _mm(p, p)
            tinv = tinv + _mm(tinv, p)
        blk = GDN_BASE
        while blk < c:
            off = jnp.where((ri // (2 * blk) == ci // (2 * blk)) & (ri // blk != ci // blk), lmat, 0.0)
            tinv = tinv - _mm(_mm(tinv, off), tinv)
            blk *= 2
        x = _mm(tinv, jnp.concatenate([vb, kb * eg], axis=1))
        u = x[:, :HEAD_LANES]
        w = x[:, HEAD_LANES:]

        sb = state.astype(BF16)
        v_new = u - _dot(w.astype(BF16), sb)
        vnb = v_new.astype(BF16)
        out = _dot((qn * eg).astype(BF16), sb) + _dot(attn.astype(BF16), vnb)
        k_tail = (kn * jnp.exp(gl - gc)).astype(BF16)
        state = state * jnp.exp(gl) + _dot(k_tail, vnb, ((0,), (0,)))

        ms = jnp.mean(out * out, axis=-1, keepdims=True)
        y = (out * lax.rsqrt(ms + EPS)) * gn
        gate = ng_ref[0, 0, 0, pl.ds(r0, c), :]
        o_ref[0, pl.ds(r0, c), :] = (y * _silu(gate)).astype(o_ref.dtype)
        return state

    lax.fori_loop(0, n_chunks, chunk, jnp.zeros((DN_HEAD_DIM, DN_HEAD_DIM), F32))


def _gated_deltanet(proj_main, tail, tail_t, conv_w, a_log_b, dt_bias_b, g_gnorm, batch, seq):
    c = GDN_C
    n_chunks = seq // c
    assert n_chunks <= 16
    arow = tail_t.reshape(16, batch, n_chunks, c)
    head_spec = lambda g: pl.BlockSpec((1, 1, 1, seq, HEAD_LANES), lambda b, h: (g, b, h, 0, 0))
    cw_spec = lambda g: pl.BlockSpec((CONV_K, HEAD_LANES), lambda b, h: (0, g * DN_HEADS + h))
    vec_spec = pl.BlockSpec((1, 1, HEAD_LANES), lambda b, h: (h, 0, 0))
    return pl.pallas_call(
        _gdn_kernel,
        grid=(batch, DN_HEADS),
        in_specs=[
            head_spec(3), head_spec(4), head_spec(5), head_spec(6),
            pl.BlockSpec((seq, HEAD_LANES), lambda b, h: (b, 0)),
            pl.BlockSpec((1, 1, n_chunks, c), lambda b, h: (DN_HEADS + h, b, 0, 0)),
            cw_spec(0), cw_spec(1), cw_spec(2),
            vec_spec, vec_spec,
            pl.BlockSpec((1, DN_HEAD_DIM), lambda b, h: (0, 0)),
        ],
        out_specs=pl.BlockSpec((1, seq, HEAD_LANES), lambda b, h: (b, 0, h)),
        out_shape=jax.ShapeDtypeStruct((batch, seq, DN_HEADS * DN_HEAD_DIM), BF16),
        scratch_shapes=[pltpu.VMEM((16, c), F32)],
        compiler_params=_cparams(("arbitrary", "arbitrary")),
        name="gated_deltanet",
    )(proj_main, proj_main, proj_main, proj_main, tail, arow, conv_w, conv_w, conv_w,
      a_log_b, dt_bias_b, g_gnorm)


OUT_TM = 512
OUT_TK = 512
EPI_ROWS = 256


def _dot_f32(a, b_parts):
    a_hi, a_mid, a_lo = _split3(a)
    b_hi, b_mid, b_lo = b_parts
    acc = _dot(a_hi, b_hi)
    acc += _dot(a_hi, b_mid) + _dot(a_mid, b_hi)
    acc += _dot(a_hi, b_lo) + _dot(a_mid, b_mid) + _dot(a_lo, b_hi)
    return acc


def _outproj_kernel(od_ref, on_ref, w_ref, x_ref, g1_ref, gf_ref, sc_ref, sh_ref, wr_ref, br_ref,
                    x1_ref, hn2_ref, idx_ref, gate_ref, rank_ref, cnt_ref, acc_ref, mh_ref, carry_ref):
    i = pl.program_id(0)
    k = pl.program_id(1)
    nk = pl.num_programs(1)
    tm = OUT_TM

    @pl.when((i == 0) & (k == 0))
    def _():
        carry_ref[...] = jnp.zeros_like(carry_ref)

    lhs = jnp.where(k < nk // 2, od_ref[...], on_ref[...])
    part = _dot(lhs, w_ref[...].astype(BF16))

    @pl.when(k == 0)
    def _():
        acc_ref[...] = part

    @pl.when(k > 0)
    def _():
        acc_ref[...] += part

    @pl.when(k == nk - 1)
    def _():
        g1 = g1_ref[0]
        gf = gf_ref[...]
        sc = sc_ref[0]
        sh = sh_ref[0]
        wr_parts = _split3(wr_ref[...])
        br = br_ref[...]
        lane = lax.broadcasted_iota(jnp.int32, (EPI_ROWS, HEAD_LANES), 1)
        for r in range(0, tm, EPI_ROWS):
            rows = slice(r, r + EPI_ROWS)
            x1 = x_ref[rows, :] + g1 * acc_ref[rows, :]
            x1_ref[rows, :] = x1
            hn2 = _norm_modulate(x1, gf, sc, sh)
            hn2_ref[rows, :] = hn2
            logits = _dot_f32(hn2, wr_parts) + br
            l = jnp.where(lane < N_EXPERTS, logits, -jnp.inf)
            vals, idxs = [], []
            for _ in range(TOP_K):
                m = jnp.max(l, axis=-1, keepdims=True)
                ix = jnp.min(jnp.where(l == m, lane, HEAD_LANES), axis=-1, keepdims=True)
                vals.append(m)
                idxs.append(ix)
                l = jnp.where(lane == ix, -jnp.inf, l)
            es = [jnp.exp(v - vals[0]) for v in vals]
            den = (es[0] + es[1]) + (es[2] + es[3])
            idx_o = jnp.zeros((EPI_ROWS, HEAD_LANES), jnp.int32)
            gate_o = jnp.zeros((EPI_ROWS, HEAD_LANES), F32)
            hot = jnp.zeros((EPI_ROWS, HEAD_LANES), F32)
            for kk in range(TOP_K):
                idx_o = jnp.where(lane == kk, idxs[kk], idx_o)
                gate_o = jnp.where(lane == kk, es[kk] / den, gate_o)
                hot = jnp.where(lane == idxs[kk], 1.0, hot)
            idx_ref[rows, :] = idx_o
            gate_ref[rows, :] = gate_o
            mh_ref[rows, :] = hot.astype(BF16)

        ri = lax.broadcasted_iota(jnp.int32, (tm, tm), 0)
        ci = lax.broadcasted_iota(jnp.int32, (tm, tm), 1)
        mh = mh_ref[...]
        carry = carry_ref[0:1, :]
        cum = _dot(_ones_where(ci < ri), mh) + carry
        lane_t = lax.broadcasted_iota(jnp.int32, (tm, HEAD_LANES), 1)
        idx_t = idx_ref[...]
        rank_o = jnp.zeros((tm, HEAD_LANES), F32)
        for kk in range(TOP_K):
            ik = idx_t[:, kk:kk + 1]
            rk = jnp.sum(jnp.where(lane_t == ik, cum, 0.0), axis=-1, keepdims=True)
            rank_o = jnp.where(lane_t == kk, rk, rank_o)
        rank_ref[...] = rank_o.astype(jnp.int32)
        total = carry + jnp.sum(mh.astype(F32), axis=0, keepdims=True)
        carry_ref[...] = jnp.broadcast_to(total, carry_ref.shape)
        cnt_ref[...] = jnp.broadcast_to(total, cnt_ref.shape).astype(jnp.int32)


def _outproj_router(o_diff, o_dn, w_out, x2, gate1, g_ffn, scale2, shift2, w_router_pad, b_router_pad, seq):
    n_tok, d = x2.shape
    tm, tk = OUT_TM, OUT_TK
    tiles_per_batch = seq // tm
    nk = d // tk
    half = nk // 2
    vec3 = pl.BlockSpec((1, 1, d), lambda i, k: (i // tiles_per_batch, 0, 0))
    row_out = lambda dt: jax.ShapeDtypeStruct((n_tok, HEAD_LANES), dt)
    return pl.pallas_call(
        _outproj_kernel,
        grid=(n_tok // tm, nk),
        in_specs=[
            pl.BlockSpec((tm, tk), lambda i, k: (i, jnp.minimum(k, half - 1))),
            pl.BlockSpec((tm, tk), lambda i, k: (i, jnp.maximum(k - half, 0))),
            pl.BlockSpec((tk, d), lambda i, k: (k, 0)),
            pl.BlockSpec((tm, d), lambda i, k: (i, 0)),
            vec3,
            pl.BlockSpec((1, d), lambda i, k: (0, 0)),
            vec3, vec3,
            pl.BlockSpec((d, HEAD_LANES), lambda i, k: (0, 0)),
            pl.BlockSpec((1, HEAD_LANES), lambda i, k: (0, 0)),
        ],
        out_specs=[
            pl.BlockSpec((tm, d), lambda i, k: (i, 0)),
            pl.BlockSpec((tm, d), lambda i, k: (i, 0)),
            pl.BlockSpec((tm, HEAD_LANES), lambda i, k: (i, 0)),
            pl.BlockSpec((tm, HEAD_LANES), lambda i, k: (i, 0)),
            pl.BlockSpec((tm, HEAD_LANES), lambda i, k: (i, 0)),
            pl.BlockSpec((8, HEAD_LANES), lambda i, k: (0, 0)),
        ],
        out_shape=[
            jax.ShapeDtypeStruct((n_tok, d), F32),
            jax.ShapeDtypeStruct((n_tok, d), F32),
            row_out(jnp.int32), row_out(F32), row_out(jnp.int32),
            jax.ShapeDtypeStruct((8, HEAD_LANES), jnp.int32),
        ],
        scratch_shapes=[pltpu.VMEM((tm, d), F32), pltpu.VMEM((tm, HEAD_LANES), BF16),
                        pltpu.VMEM((8, HEAD_LANES), F32)],
        compiler_params=_cparams(("arbitrary", "arbitrary")),
        name="outproj_router",
    )(o_diff, o_dn, w_out, x2, gate1, g_ffn, scale2, shift2, w_router_pad, b_router_pad)


MOE_TM = 1024
MOE_SUB = 256
MOE_TF = 512
MOE_TN = 512
DSP_TM = 256


def _row_copy(src_hbm, row, dst_vmem, slot, sem):
    return pltpu.make_async_copy(src_hbm.at[pl.ds(row, 1)], dst_vmem.at[pl.ds(slot, 1)], sem)


def _dispatch_kernel(sbr_ref, sbm_ref, dest_ref, hn2_ref, xs_hbm, zero_ref, sem):
    i = pl.program_id(0)
    tm = DSP_TM

    @pl.when(i == 0)
    def _():
        zero_ref[...] = jnp.zeros_like(zero_ref)
        n_sb = sbr_ref.shape[0]

        def pad_copies(s, fn):
            nv = sbr_ref[s]
            nv8 = (nv + 7) // 8 * 8
            for t in range(7):
                @pl.when(nv + t < nv8)
                def _():
                    fn(pltpu.make_async_copy(zero_ref.at[pl.ds(0, 1)],
                                             xs_hbm.at[pl.ds(s * MOE_TM + nv + t, 1)], sem))

            pad8 = (sbm_ref[s] - nv8) // 8
            bit = MOE_SUB // 16
            while bit >= 1:
                rows = 8 * bit
                start = pl.multiple_of(s * MOE_TM + nv8 + 8 * (pad8 & ~(2 * bit - 1)), 8)

                @pl.when((pad8 & bit) != 0)
                def _():
                    fn(pltpu.make_async_copy(zero_ref.at[pl.ds(0, rows)],
                                             xs_hbm.at[pl.ds(start, rows)], sem))

                bit //= 2
            return 0

        lax.fori_loop(0, n_sb, lambda s, c: c + pad_copies(s, lambda cp: cp.start()), 0)
        lax.fori_loop(0, n_sb, lambda s, c: c + pad_copies(s, lambda cp: cp.wait()), 0)

    def issue(r, carry):
        for kk in range(TOP_K):
            pltpu.make_async_copy(hn2_ref.at[pl.ds(r, 1)],
                                  xs_hbm.at[pl.ds(dest_ref[0, 0, r * TOP_K + kk], 1)], sem).start()
        return carry

    lax.fori_loop(0, tm, issue, 0)

    def wait(r, carry):
        for kk in range(TOP_K):
            pltpu.make_async_copy(hn2_ref.at[pl.ds(0, 1)], xs_hbm.at[pl.ds(0, 1)], sem).wait()
        return carry

    lax.fori_loop(0, tm, wait, 0)


def _dispatch(sb_rows, sb_m, dest_tiles, hn2, n_rows):
    n_tok, d = hn2.shape
    tm = DSP_TM
    grid_spec = pltpu.PrefetchScalarGridSpec(
        num_scalar_prefetch=2,
        grid=(n_tok // tm,),
        in_specs=[
            pl.BlockSpec((1, 1, tm * TOP_K), lambda i, sbr, sbm: (i, 0, 0), memory_space=pltpu.SMEM),
            pl.BlockSpec((tm, d), lambda i, sbr, sbm: (i, 0)),
        ],
        out_specs=pl.BlockSpec(memory_space=pl.ANY),
        scratch_shapes=[pltpu.VMEM((MOE_SUB // 2, d), F32), pltpu.SemaphoreType.DMA(())],
    )
    return pl.pallas_call(
        _dispatch_kernel,
        grid_spec=grid_spec,
        out_shape=jax.ShapeDtypeStruct((n_rows, d), F32),
        compiler_params=_cparams(("arbitrary",)),
        name="moe_dispatch",
    )(sb_rows, sb_m, dest_tiles, hn2)


def _moe_up_kernel(sbe_ref, sbm_ref, nused_ref, x_ref, wg_ref, wu_ref, bg_ref, bu_ref, h_ref, xb_ref):
    s = pl.program_id(0)
    f = pl.program_id(1)
    m = sbm_ref[s]

    for r in range(0, MOE_TM, MOE_SUB):
        @pl.when((f == 0) & (r < m))
        def _():
            xb_ref[r:r + MOE_SUB, :] = x_ref[r:r + MOE_SUB, :].astype(BF16)

    for rows in range(MOE_SUB, MOE_TM + 1, MOE_SUB):
        @pl.when(m == rows)
        def _():
            wg = wg_ref[0].astype(BF16)
            wu = wu_ref[0].astype(BF16)
            xq = xb_ref[0:rows, :]
            gate = jnp.minimum(_dot(xq, wg) + bg_ref[0], SWIGLU_LIMIT)
            up = jnp.clip(_dot(xq, wu) + bu_ref[0], -SWIGLU_LIMIT, SWIGLU_LIMIT)
            glu = gate * jax.nn.sigmoid(gate * SWIGLU_ALPHA)
            h_ref[0:rows, :] = ((up + 1.0) * glu).astype(BF16)
            if rows < MOE_TM:
                h_ref[rows:MOE_TM, :] = jnp.zeros((MOE_TM - rows, h_ref.shape[1]), BF16)


def _moe_up(sb_expert, sb_m, n_used, xs, w_gate_up, b_gate_up):
    n_sb = sb_expert.shape[0]
    d = xs.shape[1]
    n_exp, _, f2 = w_gate_up.shape
    f_dim = f2 // 2
    nf = f_dim // MOE_TF

    def f_eff(s, f, nu):
        return jnp.where(s < nu[0], f, nf - 1)

    def s_eff(s, nu):
        return jnp.minimum(s, nu[0] - 1)

    grid_spec = pltpu.PrefetchScalarGridSpec(
        num_scalar_prefetch=3,
        grid=(n_sb, nf),
        in_specs=[
            pl.BlockSpec((MOE_TM, d), lambda s, f, sbe, sbm, nu: (s_eff(s, nu), 0)),
            pl.BlockSpec((1, d, MOE_TF), lambda s, f, sbe, sbm, nu: (sbe[s], 0, f_eff(s, f, nu))),
            pl.BlockSpec((1, d, MOE_TF), lambda s, f, sbe, sbm, nu: (sbe[s], 0, nf + f_eff(s, f, nu))),
            pl.BlockSpec((1, 1, MOE_TF), lambda s, f, sbe, sbm, nu: (sbe[s], 0, f_eff(s, f, nu))),
            pl.BlockSpec((1, 1, MOE_TF), lambda s, f, sbe, sbm, nu: (sbe[s], 0, nf + f_eff(s, f, nu))),
        ],
        out_specs=pl.BlockSpec((MOE_TM, MOE_TF), lambda s, f, sbe, sbm, nu: (s_eff(s, nu), f_eff(s, f, nu))),
        scratch_shapes=[pltpu.VMEM((MOE_TM, d), BF16)],
    )
    b3 = b_gate_up.reshape(n_exp, 1, f2)
    return pl.pallas_call(
        _moe_up_kernel,
        grid_spec=grid_spec,
        out_shape=jax.ShapeDtypeStruct((n_sb * MOE_TM, f_dim), BF16),
        compiler_params=_cparams(("arbitrary", "arbitrary")),
        name="moe_up",
    )(sb_expert, sb_m, n_used, xs, w_gate_up, w_gate_up, b3, b3)


def _moe_down_kernel(sbe_ref, sbm_ref, nused_ref, h_ref, wd_ref, bd_ref, o_ref):
    s = pl.program_id(0)
    m = sbm_ref[s]

    for rows in range(MOE_SUB, MOE_TM + 1, MOE_SUB):
        @pl.when(m == rows)
        def _():
            o_ref[0:rows, :] = _dot(h_ref[0:rows, :], wd_ref[0].astype(BF16)) + bd_ref[0]
            if rows < MOE_TM:
                o_ref[rows:MOE_TM, :] = jnp.zeros((MOE_TM - rows, o_ref.shape[1]), F32)


def _moe_down(sb_expert, sb_m, n_used, h, w_down, b_down):
    n_sb = sb_expert.shape[0]
    n_exp, f_dim, d = w_down.shape
    nn = d // MOE_TN

    def n_eff(s, n, nu):
        return jnp.where(s < nu[0], n, nn - 1)

    def s_eff(s, nu):
        return jnp.minimum(s, nu[0] - 1)

    grid_spec = pltpu.PrefetchScalarGridSpec(
        num_scalar_prefetch=3,
        grid=(n_sb, nn),
        in_specs=[
            pl.BlockSpec((MOE_TM, f_dim), lambda s, n, sbe, sbm, nu: (s_eff(s, nu), 0)),
            pl.BlockSpec((1, f_dim, MOE_TN), lambda s, n, sbe, sbm, nu: (sbe[s], 0, n_eff(s, n, nu))),
            pl.BlockSpec((1, 1, MOE_TN), lambda s, n, sbe, sbm, nu: (sbe[s], 0, n_eff(s, n, nu))),
        ],
        out_specs=pl.BlockSpec((MOE_TM, MOE_TN), lambda s, n, sbe, sbm, nu: (s_eff(s, nu), n_eff(s, n, nu))),
    )
    return pl.pallas_call(
        _moe_down_kernel,
        grid_spec=grid_spec,
        out_shape=jax.ShapeDtypeStruct((n_sb * MOE_TM, d), F32),
        compiler_params=_cparams(("arbitrary", "arbitrary")),
        name="moe_down",
    )(sb_expert, sb_m, n_used, h, w_down, b_down.reshape(n_exp, 1, d))


CMB_TM = 256


def _combine_kernel(dest_ref, ys_hbm, x1_ref, gate_ref, g2_ref, gfin_ref, o_ref, buf_ref, sem):
    tm = CMB_TM

    def issue(r, carry):
        for kk in range(TOP_K):
            _row_copy(ys_hbm, dest_ref[0, 0, r * TOP_K + kk], buf_ref.at[kk], r, sem).start()
        return carry

    lax.fori_loop(0, tm, issue, 0)

    def wait(r, carry):
        for kk in range(TOP_K):
            _row_copy(ys_hbm, 0, buf_ref.at[kk], 0, sem).wait()
        return carry

    lax.fori_loop(0, tm, wait, 0)

    gates = gate_ref[...]
    y = gates[:, 0:1] * buf_ref[0]
    for kk in range(1, TOP_K):
        y = y + gates[:, kk:kk + 1] * buf_ref[kk]
    x2 = x1_ref[...] + g2_ref[0] * y
    ms = jnp.mean(x2 * x2, axis=-1, keepdims=True)
    o_ref[...] = (x2 * lax.rsqrt(ms + EPS)) * gfin_ref[...]


def _combine(dest_tiles, ys, x1, gates, gate2, g_final, seq):
    n_tok, d = x1.shape
    tm = CMB_TM
    tiles_per_batch = seq // tm
    return pl.pallas_call(
        _combine_kernel,
        grid=(n_tok // tm,),
        in_specs=[
            pl.BlockSpec((1, 1, tm * TOP_K), lambda i: (i, 0, 0), memory_space=pltpu.SMEM),
            pl.BlockSpec(memory_space=pl.ANY),
            pl.BlockSpec((tm, d), lambda i: (i, 0)),
            pl.BlockSpec((tm, HEAD_LANES), lambda i: (i, 0)),
            pl.BlockSpec((1, 1, d), lambda i: (i // tiles_per_batch, 0, 0)),
            pl.BlockSpec((1, d), lambda i: (0, 0)),
        ],
        out_specs=pl.BlockSpec((tm, d), lambda i: (i, 0)),
        out_shape=jax.ShapeDtypeStruct((n_tok, d), F32),
        scratch_shapes=[pltpu.VMEM((TOP_K, tm, d), F32), pltpu.SemaphoreType.DMA(())],
        compiler_params=_cparams(("arbitrary",)),
        name="moe_combine_norm",
    )(dest_tiles, ys, x1, gates, gate2, g_final)


def kernel(x, c, w_ada, b_ada, g_norm_mix, g_norm_ffn, g_norm_final, w_in, conv_w, rel_bias,
           lambda_q1, lambda_k1, lambda_q2, lambda_k2, g_subln, a_log, dt_bias, g_gnorm, w_out,
           w_router, b_router, w_gate_up, b_gate_up, w_down, b_down):
    batch, seq, d = x.shape
    n_tok = batch * seq
    depth = w_ada.shape[0]
    assert depth == 1 and d == D_MODEL
    x2 = x.reshape(n_tok, d)
    bias_tiles, far = _bias_tiles(rel_bias)
    for l in range(depth):
        c_pad = jnp.zeros((16, d), F32).at[:batch].set(c)
        mod = _adaln(c_pad, w_ada[l], b_ada[l][None, :])[:batch]
        shift1, scale1, gate1, shift2, scale2, gate2 = [m.reshape(batch, 1, d) for m in jnp.split(mod, 6, axis=-1)]

        w_tail = jnp.pad(w_in[l][:, MAIN_COLS:], ((0, 0), (0, HEAD_LANES - 2 * DN_HEADS)))
        w_tail_t = w_in[l][:, MAIN_COLS:].T
        proj_main, tail, tail_t = _inproj(x2, g_norm_mix[l][None, :], scale1, shift1, w_in[l], w_tail, w_tail_t,
                                          batch, seq)
        o_diff = _diff_attention(proj_main, bias_tiles, far, lambda_q1[l][None, :], lambda_k1[l][None, :],
                                 lambda_q2[l][None, :], lambda_k2[l][None, :], g_subln[l][None, :], batch, seq)
        a_log_b = jnp.broadcast_to(a_log[l][:, None, None], (DN_HEADS, 1, HEAD_LANES))
        dt_bias_b = jnp.broadcast_to(dt_bias[l][:, None, None], (DN_HEADS, 1, HEAD_LANES))
        o_dn = _gated_deltanet(proj_main, tail, tail_t, conv_w[l], a_log_b, dt_bias_b, g_gnorm[l][None, :],
                               batch, seq)

        w_router_pad = jnp.pad(w_router[l], ((0, 0), (0, HEAD_LANES - N_EXPERTS)))
        b_router_pad = jnp.pad(b_router[l], (0, HEAD_LANES - N_EXPERTS))[None, :]
        x1, hn2, top_idx, gates, rank, counts = _outproj_router(
            o_diff.reshape(n_tok, -1), o_dn.reshape(n_tok, -1), w_out[l], x2, gate1, g_norm_ffn[l][None, :],
            scale2, shift2, w_router_pad, b_router_pad, seq)

        n_assign = n_tok * TOP_K
        n_sb = n_assign // MOE_TM + N_EXPERTS
        cnt = counts[0, :N_EXPERTS]
        sb_per_e = (cnt + MOE_TM - 1) // MOE_TM
        rows_per_sb = (cnt + jnp.maximum(sb_per_e, 1) - 1) // jnp.maximum(sb_per_e, 1)
        sb_end = jnp.cumsum(sb_per_e)
        sb_start = sb_end - sb_per_e
        n_used = sb_end[-1]
        s_ids = jnp.arange(n_sb, dtype=jnp.int32)
        e_ids = jnp.arange(N_EXPERTS, dtype=jnp.int32)
        s_clamped = jnp.minimum(s_ids, n_used - 1)
        e_of_s = jnp.sum((sb_end[None, :] <= s_clamped[:, None]).astype(jnp.int32), axis=1)
        e_of_s = jnp.minimum(e_of_s, N_EXPERTS - 1).astype(jnp.int32)
        onehot_s = (e_of_s[:, None] == e_ids[None, :]).astype(jnp.int32)
        j_of_s = s_ids - jnp.sum(onehot_s * sb_start[None, :], axis=1)
        per_s = jnp.sum(onehot_s * rows_per_sb[None, :], axis=1)
        cnt_s = jnp.sum(onehot_s * cnt[None, :], axis=1)
        sb_rows = jnp.where(s_ids < n_used, jnp.clip(cnt_s - j_of_s * per_s, 0, per_s), 0).astype(jnp.int32)
        sb_m = ((sb_rows + MOE_SUB - 1) // MOE_SUB * MOE_SUB).astype(jnp.int32)
        idx4 = top_idx[:, :TOP_K]
        onehot_a = (idx4[:, :, None] == e_ids[None, None, :]).astype(jnp.int32)
        start_a = jnp.sum(onehot_a * sb_start[None, None, :], axis=-1)
        per_a = jnp.maximum(jnp.sum(onehot_a * rows_per_sb[None, None, :], axis=-1), 1)
        rank4 = rank[:, :TOP_K]
        dest = ((start_a + rank4 // per_a) * MOE_TM + rank4 % per_a).astype(jnp.int32)
        n_used1 = n_used.reshape(1).astype(jnp.int32)

        xs = _dispatch(sb_rows, sb_m, dest.reshape(n_tok // DSP_TM, 1, DSP_TM * TOP_K), hn2, n_sb * MOE_TM)
        hmid = _moe_up(e_of_s, sb_m, n_used1, xs, w_gate_up[l], b_gate_up[l])
        ys = _moe_down(e_of_s, sb_m, n_used1, hmid, w_down[l], b_down[l])
        x2 = _combine(dest.reshape(n_tok // CMB_TM, 1, CMB_TM * TOP_K), ys, x1, gates, gate2,
                      g_norm_final[None, :], seq)
    return x2.reshape(batch, seq, d)
```

```python
import functools
import math

import jax
import jax.numpy as jnp
import numpy as np
from jax import lax
from jax.experimental import pallas as pl
from jax.experimental.pallas import tpu as pltpu

F32 = jnp.float32
BF16 = jnp.bfloat16

D_MODEL = 2048
DIFF_HEADS = 8
DIFF_HEAD_DIM = 64
DIFF_V_DIM = 128
DN_HEADS = 8
DN_HEAD_DIM = 128
HEAD_LANES = 128
CONV_K = 4
N_BUCKETS = 32
MAX_DISTANCE = 128
N_EXPERTS = 32
TOP_K = 4
D_EXPERT = 2048
SWIGLU_LIMIT = 7.0
SWIGLU_ALPHA = 1.702
EPS = 1e-6
LAMBDA_INIT = 0.8 - 0.6 * math.exp(-0.3 * 0)
N_MAIN_GROUPS = 7
MAIN_COLS = N_MAIN_GROUPS * 1024

V7X_VMEM_BYTES = 64 * 1024 * 1024
VMEM_LIMIT = 56 * 1024 * 1024

NEG_BIG = -1e30


def _cparams(sem):
    return pltpu.CompilerParams(dimension_semantics=sem, vmem_limit_bytes=VMEM_LIMIT)


def _split3(a):
    hi = a.astype(BF16)
    r1 = a - hi.astype(F32)
    mid = r1.astype(BF16)
    lo = (r1 - mid.astype(F32)).astype(BF16)
    return hi, mid, lo


def _dot(a, b, dims=None):
    if dims is None:
        return jnp.dot(a, b, preferred_element_type=F32)
    return lax.dot_general(a, b, (dims, ((), ())), preferred_element_type=F32)


def _dot_exact_lhs(a, b_bf16, dims=None):
    hi, mid, lo = _split3(a)
    return (_dot(hi, b_bf16, dims) + _dot(mid, b_bf16, dims)) + _dot(lo, b_bf16, dims)


def _dot_exact_rhs(a_bf16, b, dims=None):
    hi, mid, lo = _split3(b)
    return (_dot(a_bf16, hi, dims) + _dot(a_bf16, mid, dims)) + _dot(a_bf16, lo, dims)


NT = ((1,), (1,))


def _ones_where(mask):
    return jnp.where(mask, 1.0, 0.0).astype(BF16)


ADA_TN = 1024


def _adaln_kernel(c_ref, w_ref, b_ref, o_ref):
    c = c_ref[...]
    sc = c * jax.nn.sigmoid(c)
    a_hi, a_mid, a_lo = _split3(sc)
    w_hi, w_mid, w_lo = _split3(w_ref[...])
    acc = _dot(a_hi, w_hi)
    acc += _dot(a_hi, w_mid) + _dot(a_mid, w_hi)
    acc += _dot(a_hi, w_lo) + _dot(a_mid, w_mid) + _dot(a_lo, w_hi)
    o_ref[...] = acc + b_ref[...]


def _adaln(c_pad, w, b):
    n = w.shape[1]
    return pl.pallas_call(
        _adaln_kernel,
        grid=(n // ADA_TN,),
        in_specs=[
            pl.BlockSpec(c_pad.shape, lambda j: (0, 0)),
            pl.BlockSpec((w.shape[0], ADA_TN), lambda j: (0, j)),
            pl.BlockSpec((1, ADA_TN), lambda j: (0, j)),
        ],
        out_specs=pl.BlockSpec((c_pad.shape[0], ADA_TN), lambda j: (0, j)),
        out_shape=jax.ShapeDtypeStruct((c_pad.shape[0], n), F32),
        compiler_params=_cparams(("arbitrary",)),
        name="adaln_mod",
    )(c_pad, w, b)


CAST_ROWS = 256


def _cast_kernel(w_ref, o_ref):
    o_ref[...] = w_ref[...].astype(BF16)


def _to_bf16(w, n_cols):
    k = w.shape[0]
    return pl.pallas_call(
        _cast_kernel,
        grid=(k // CAST_ROWS,),
        in_specs=[pl.BlockSpec((CAST_ROWS, n_cols), lambda i: (i, 0))],
        out_specs=pl.BlockSpec((CAST_ROWS, n_cols), lambda i: (i, 0)),
        out_shape=jax.ShapeDtypeStruct((k, n_cols), BF16),
        compiler_params=_cparams(("arbitrary",)),
        name="weights_to_bf16",
    )(w)


INPROJ_TM = 1024
INPROJ_TN = 1024
NORM_ROWS = 256


def _norm_modulate(x, g, scale, shift):
    ms = jnp.mean(x * x, axis=-1, keepdims=True)
    y = (x * lax.rsqrt(ms + EPS)) * g
    return y * (1.0 + scale) + shift


def _inproj_kernel(x_ref, g_ref, sc_ref, sh_ref, w_ref, wt_ref, wtt_ref, o_ref, t_ref, tt_ref, hn_ref):
    j = pl.program_id(1)

    @pl.when(j == 0)
    def _():
        g = g_ref[...]
        sc = sc_ref[0]
        sh = sh_ref[0]
        for r in range(0, INPROJ_TM, NORM_ROWS):
            hn = _norm_modulate(x_ref[r:r + NORM_ROWS, :], g, sc, sh)
            hn_ref[r:r + NORM_ROWS, :] = hn.astype(BF16)
        hnb = hn_ref[...]
        t_ref[...] = _dot(hnb, wt_ref[...].astype(BF16))
        tt_ref[...] = _dot(wtt_ref[...].astype(BF16), hnb, NT)

    acc = _dot(hn_ref[...], w_ref[...])
    for hh in range(INPROJ_TN // HEAD_LANES):
        o_ref[0, 0, hh] = acc[:, hh * HEAD_LANES:(hh + 1) * HEAD_LANES]


def _inproj(x2, g_mix, scale1, shift1, w_in, w_tail, w_tail_t, batch, seq):
    n_tok, d = x2.shape
    tiles_per_batch = seq // INPROJ_TM
    heads_per_tile = INPROJ_TN // HEAD_LANES
    tiles_per_group = 1024 // INPROJ_TN
    grid = (n_tok // INPROJ_TM, MAIN_COLS // INPROJ_TN)
    return pl.pallas_call(
        _inproj_kernel,
        grid=grid,
        in_specs=[
            pl.BlockSpec((INPROJ_TM, d), lambda i, j: (i, 0)),
            pl.BlockSpec((1, d), lambda i, j: (0, 0)),
            pl.BlockSpec((1, 1, d), lambda i, j: (i // tiles_per_batch, 0, 0)),
            pl.BlockSpec((1, 1, d), lambda i, j: (i // tiles_per_batch, 0, 0)),
            pl.BlockSpec((d, INPROJ_TN), lambda i, j: (0, j)),
            pl.BlockSpec((d, HEAD_LANES), lambda i, j: (0, 0)),
            pl.BlockSpec((16, d), lambda i, j: (0, 0)),
        ],
        out_specs=[
            pl.BlockSpec((1, 1, heads_per_tile, INPROJ_TM, HEAD_LANES),
                         lambda i, j: (j // tiles_per_group, i // tiles_per_batch, j % tiles_per_group,
                                       i % tiles_per_batch, 0)),
            pl.BlockSpec((INPROJ_TM, HEAD_LANES), lambda i, j: (i, 0)),
            pl.BlockSpec((16, INPROJ_TM), lambda i, j: (0, i)),
        ],
        out_shape=[
            jax.ShapeDtypeStruct((N_MAIN_GROUPS, batch, 8, seq, HEAD_LANES), F32),
            jax.ShapeDtypeStruct((n_tok, HEAD_LANES), F32),
            jax.ShapeDtypeStruct((16, n_tok), F32),
        ],
        scratch_shapes=[pltpu.VMEM((INPROJ_TM, d), BF16)],
        compiler_params=_cparams(("arbitrary", "arbitrary")),
        name="norm_inproj",
    )(x2, g_mix, scale1, shift1, w_in, w_tail, w_tail_t)


ATT_T = 256
ATT_ROWS = 128


def _t5_bucket(n):
    n = jnp.maximum(n, 0)
    max_exact = N_BUCKETS // 2
    nf = jnp.maximum(n, max_exact).astype(F32)
    large = max_exact + (jnp.log(nf / max_exact) / math.log(MAX_DISTANCE / max_exact)
                         * (N_BUCKETS - max_exact)).astype(jnp.int32)
    large = jnp.minimum(large, N_BUCKETS - 1)
    return jnp.where(n < max_exact, n, large)


def _bias_tiles(rel_bias):
    assert ATT_T >= MAX_DISTANCE
    t = ATT_T
    table = rel_bias.astype(F32)
    n_heads = table.shape[1]
    tiles = []
    for d in range(2):
        dist = (d * t + t) - jnp.arange(2 * t, dtype=jnp.int32)
        w = jnp.where(dist[:, None] >= 0, table[_t5_bucket(dist)], F32(NEG_BIG)).T
        w_ext = jnp.concatenate([w, jnp.zeros((n_heads, 1), F32)], axis=1)
        skew = jnp.tile(w_ext, (1, t))[:, :t * 2 * t].reshape(n_heads, t, 2 * t)
        tiles.append(skew[:, :, t:])
    far = jnp.broadcast_to(table[N_BUCKETS - 1][:, None, None], (n_heads, t, t))
    return jnp.stack(tiles + [far], axis=1)


def _attn_kernel(q_ref, k_ref, v_ref, bias_ref, lq1_ref, lk1_ref, lq2_ref, lk2_ref, gs_ref,
                 o_ref, kb_ref, vb_ref, s0_ref, s1_ref, p_ref, m_ref, l_ref, acc_ref):
    i = pl.program_id(2)
    t = ATT_T
    rows2 = 2 * t

    @pl.when(i == 0)
    def _():
        kb_ref[...] = k_ref[0, 0, 0].astype(BF16)
        vb_ref[...] = v_ref[0, 0, 0].astype(BF16)

    q = q_ref[0, 0, 0] * (DIFF_HEAD_DIM ** -0.5)
    lane = lax.broadcasted_iota(jnp.int32, q.shape, 1)
    q1 = jnp.where(lane < DIFF_HEAD_DIM, q, 0.0)
    q2 = jnp.where(lane >= DIFF_HEAD_DIM, q, 0.0)
    q12 = jnp.concatenate([q1, q2], axis=0).astype(BF16)

    m_ref[...] = jnp.full(m_ref.shape, -jnp.inf, F32)
    l_ref[...] = jnp.zeros(l_ref.shape, F32)
    acc_ref[...] = jnp.zeros(acc_ref.shape, F32)

    def scores(j):
        return _dot(q12, kb_ref[pl.ds(pl.multiple_of(j * t, t), t), :], NT)

    def block(j, s_cur_ref, s_next_ref):
        s_next_ref[...] = scores(jnp.minimum(j + 1, i))
        d = jnp.minimum(i - j, 2)
        for r in range(0, rows2, ATT_ROWS):
            rows = slice(r, r + ATT_ROWS)
            brows = slice(r % t, r % t + ATT_ROWS)
            halves = [s_cur_ref[rows, c:c + HEAD_LANES] + bias_ref[0, d, brows, c:c + HEAD_LANES]
                      for c in range(0, t, HEAD_LANES)]
            m_old = m_ref[rows, :]
            m_new = jnp.maximum(m_old, jnp.max(functools.reduce(jnp.maximum, halves), axis=-1, keepdims=True))
            alpha = jnp.exp(m_old - m_new)
            ps = [jnp.exp(sh - m_new) for sh in halves]
            l_ref[rows, :] = alpha * l_ref[rows, :] + jnp.sum(functools.reduce(jnp.add, ps), axis=-1, keepdims=True)
            m_ref[rows, :] = m_new
            for c, ph in zip(range(0, t, HEAD_LANES), ps):
                p_ref[rows, c:c + HEAD_LANES] = ph.astype(BF16)
            acc_ref[rows, :] = alpha * acc_ref[rows, :]
        acc_ref[...] += _dot(p_ref[...], vb_ref[pl.ds(pl.multiple_of(j * t, t), t), :])

    s0_ref[...] = scores(0)

    def pair(pidx, carry):
        j0 = 2 * pidx
        block(j0, s0_ref, s1_ref)

        @pl.when(j0 + 1 <= i)
        def _():
            block(j0 + 1, s1_ref, s0_ref)

        return carry

    lax.fori_loop(0, (i + 2) // 2, pair, 0)

    lam = (jnp.exp(jnp.sum(lq1_ref[...] * lk1_ref[...], axis=-1, keepdims=True))
           - jnp.exp(jnp.sum(lq2_ref[...] * lk2_ref[...], axis=-1, keepdims=True)) + LAMBDA_INIT)
    o = acc_ref[0:t, :] / l_ref[0:t, :] - lam * (acc_ref[t:rows2, :] / l_ref[t:rows2, :])
    ms = jnp.mean(o * o, axis=-1, keepdims=True)
    y = (o * lax.rsqrt(ms + EPS)) * gs_ref[...]
    o_ref[0] = (y * (1.0 - LAMBDA_INIT)).astype(o_ref.dtype)


def _diff_attention(proj_main, bias_tiles, lq1, lk1, lq2, lk2, g_subln, batch, seq):
    t = ATT_T
    nq = seq // t
    lam_spec = pl.BlockSpec((1, DIFF_HEAD_DIM), lambda b, h, i: (0, 0))
    return pl.pallas_call(
        _attn_kernel,
        grid=(batch, DIFF_HEADS, nq),
        in_specs=[
            pl.BlockSpec((1, 1, 1, t, HEAD_LANES), lambda b, h, i: (0, b, h, i, 0)),
            pl.BlockSpec((1, 1, 1, seq, HEAD_LANES), lambda b, h, i: (1, b, h, 0, 0)),
            pl.BlockSpec((1, 1, 1, seq, HEAD_LANES), lambda b, h, i: (2, b, h, 0, 0)),
            pl.BlockSpec((1, 3, t, t), lambda b, h, i: (h, 0, 0, 0)),
            lam_spec, lam_spec, lam_spec, lam_spec,
            pl.BlockSpec((1, DIFF_V_DIM), lambda b, h, i: (0, 0)),
        ],
        out_specs=pl.BlockSpec((1, t, HEAD_LANES), lambda b, h, i: (b, i, h)),
        scratch_shapes=[
            pltpu.VMEM((seq, HEAD_LANES), BF16), pltpu.VMEM((seq, HEAD_LANES), BF16),
            pltpu.VMEM((2 * t, t), F32), pltpu.VMEM((2 * t, t), F32), pltpu.VMEM((2 * t, t), BF16),
            pltpu.VMEM((2 * t, HEAD_LANES), F32), pltpu.VMEM((2 * t, HEAD_LANES), F32),
            pltpu.VMEM((2 * t, DIFF_V_DIM), F32),
        ],
        out_shape=jax.ShapeDtypeStruct((batch, seq, DIFF_HEADS * DIFF_V_DIM), BF16),
        compiler_params=_cparams(("arbitrary", "arbitrary", "arbitrary")),
        name="diff_attention",
    )(proj_main, proj_main, proj_main, bias_tiles, lq1, lk1, lq2, lk2, g_subln)


GDN_C = 256
GDN_BASE = 16
GDN_HEADS_PER_STEP = 2


def _softplus(x):
    return jnp.maximum(x, 0.0) + jnp.log1p(jnp.exp(-jnp.abs(x)))


def _silu(x):
    return x * jax.nn.sigmoid(x)


def _mm(a, b):
    return _dot(a.astype(BF16), b.astype(BF16))


def _causal_conv_silu(ref, hh, r0, first, cw):
    cur = ref[0, 0, hh, pl.ds(r0, GDN_C), :]
    prev = ref[0, 0, hh, pl.ds(jnp.maximum(r0 - 8, 0), 8), :]
    prev = jnp.where(first, 0.0, prev)
    ext = jnp.concatenate([prev, cur], axis=0)
    acc = cur * cw[CONV_K - 1:CONV_K, :]
    for k in range(1, CONV_K):
        acc = acc + ext[8 - k:8 - k + GDN_C, :] * cw[CONV_K - 1 - k:CONV_K - k, :]
    return _silu(acc)


def _gdn_kernel(nq_ref, nk_ref, nv_ref, ng_ref, tail_ref, arow_ref, cwq_ref, cwk_ref, cwv_ref,
                alog_ref, dtb_ref, gn_ref, o_ref, gcr_ref):
    c = GDN_C
    n_chunks = nq_ref.shape[3] // c
    ri = lax.broadcasted_iota(jnp.int32, (c, c), 0)
    ci = lax.broadcasted_iota(jnp.int32, (c, c), 1)
    tril = ci <= ri
    strict = ci < ri
    tril_b = _ones_where(tril)
    triu_b = _ones_where(ri <= ci)
    gn = gn_ref[...]

    for hh in range(GDN_HEADS_PER_STEP):
        coef = -jnp.exp(alog_ref[hh])
        g_rows = coef[:, :1] * _softplus(arow_ref[hh, 0] + dtb_ref[hh][:, :1])
        g_rows = jnp.concatenate([g_rows, jnp.zeros((16 - n_chunks, c), F32)], axis=0)
        gcr_ref[hh] = _dot_exact_lhs(g_rows, triu_b)

    def chunk(n, states):
        return tuple(head_chunk(hh, n, states[hh]) for hh in range(GDN_HEADS_PER_STEP))

    def head_chunk(hh, n, state):
        h = pl.program_id(1) * GDN_HEADS_PER_STEP + hh
        coef = -jnp.exp(alog_ref[hh])
        dtb = dtb_ref[hh]
        lanes = slice(hh * HEAD_LANES, (hh + 1) * HEAD_LANES)
        sel_r = lax.broadcasted_iota(jnp.int32, (HEAD_LANES, 2 * HEAD_LANES), 0)
        sel_c = lax.broadcasted_iota(jnp.int32, (HEAD_LANES, 2 * HEAD_LANES), 1)
        sel = _ones_where(sel_r == jnp.where(sel_c < HEAD_LANES, h, DN_HEADS + h))
        r0 = pl.multiple_of(n * c, c)
        first = n == 0
        cq = _causal_conv_silu(nq_ref, hh, r0, first, cwq_ref[:, lanes])
        ck = _causal_conv_silu(nk_ref, hh, r0, first, cwk_ref[:, lanes])
        cv = _causal_conv_silu(nv_ref, hh, r0, first, cwv_ref[:, lanes])
        qn = (cq * lax.rsqrt(jnp.sum(cq * cq, axis=-1, keepdims=True) + EPS)) * (DN_HEAD_DIM ** -0.5)
        kn = ck * lax.rsqrt(jnp.sum(ck * ck, axis=-1, keepdims=True) + EPS)

        ba = _dot_exact_lhs(tail_ref[pl.ds(r0, c), :], sel)
        beta = jax.nn.sigmoid(ba[:, :HEAD_LANES])
        g_col = coef * _softplus(ba[:, HEAD_LANES:] + dtb)
        gc = _dot_exact_rhs(tril_b, g_col)
        gl = gc[c - 1:c, :]
        eg = jnp.exp(gc)
        g_row = gcr_ref[hh, pl.ds(n, 1), :]

        kb = kn * beta
        vb = cv * beta
        kkqk = _dot(jnp.concatenate([kb, qn], axis=0).astype(BF16), kn.astype(BF16), NT)
        diff = jnp.concatenate([gc, gc], axis=1) - g_row
        decay = jnp.exp(jnp.where(tril, diff, -jnp.inf))
        lmat = jnp.where(strict, kkqk[:c] * decay, 0.0)
        attn = kkqk[c:] * decay

        rb = ri // GDN_BASE
        cb = ci // GDN_BASE
        p = jnp.where(rb == cb, lmat, 0.0)
        tinv = jnp.where(ri == ci, 1.0, 0.0) - p
        for _ in range(int(math.log2(GDN_BASE)) - 1):
            p = _mm(p, p)
            tinv = tinv + _mm(tinv, p)
        blk = GDN_BASE
        while blk < c:
            off = jnp.where((ri // (2 * blk) == ci // (2 * blk)) & (ri // blk != ci // blk), lmat, 0.0)
            tinv = tinv - _mm(_mm(tinv, off), tinv)
            blk *= 2
        x = _mm(tinv, jnp.concatenate([vb, kb * eg], axis=1))
        u = x[:, :HEAD_LANES]
        w = x[:, HEAD_LANES:]

        sb = state.astype(BF16)
        v_new = u - _dot(w.astype(BF16), sb)
        vnb = v_new.astype(BF16)
        out = _dot((qn * eg).astype(BF16), sb) + _dot(attn.astype(BF16), vnb)
        k_tail = (kn * jnp.exp(gl - gc)).astype(BF16)
        state = state * jnp.exp(gl) + _dot(k_tail, vnb, ((0,), (0,)))

        ms = jnp.mean(out * out, axis=-1, keepdims=True)
        y = (out * lax.rsqrt(ms + EPS)) * gn
        gate = ng_ref[0, 0, hh, pl.ds(r0, c), :]
        o_ref[0, pl.ds(r0, c), lanes] = (y * _silu(gate)).astype(o_ref.dtype)
        return state

    state0 = tuple(jnp.zeros((DN_HEAD_DIM, DN_HEAD_DIM), F32) for _ in range(GDN_HEADS_PER_STEP))
    lax.fori_loop(0, n_chunks, chunk, state0)


def _gated_deltanet(proj_main, tail, tail_t, conv_w, a_log_b, dt_bias_b, g_gnorm, batch, seq):
    c = GDN_C
    hps = GDN_HEADS_PER_STEP
    n_chunks = seq // c
    assert n_chunks <= 16 and DN_HEADS % hps == 0
    arow = tail_t.reshape(16, batch, n_chunks, c)
    head_spec = lambda g: pl.BlockSpec((1, 1, hps, seq, HEAD_LANES), lambda b, h: (g, b, h, 0, 0))
    cw_spec = lambda g: pl.BlockSpec((CONV_K, hps * HEAD_LANES), lambda b, h: (0, g * (DN_HEADS // hps) + h))
    vec_spec = pl.BlockSpec((hps, 1, HEAD_LANES), lambda b, h: (h, 0, 0))
    return pl.pallas_call(
        _gdn_kernel,
        grid=(batch, DN_HEADS // hps),
        in_specs=[
            head_spec(3), head_spec(4), head_spec(5), head_spec(6),
            pl.BlockSpec((seq, HEAD_LANES), lambda b, h: (b, 0)),
            pl.BlockSpec((hps, 1, n_chunks, c), lambda b, h: (DN_HEADS // hps + h, b, 0, 0)),
            cw_spec(0), cw_spec(1), cw_spec(2),
            vec_spec, vec_spec,
            pl.BlockSpec((1, DN_HEAD_DIM), lambda b, h: (0, 0)),
        ],
        out_specs=pl.BlockSpec((1, seq, hps * HEAD_LANES), lambda b, h: (b, 0, h)),
        out_shape=jax.ShapeDtypeStruct((batch, seq, DN_HEADS * DN_HEAD_DIM), BF16),
        scratch_shapes=[pltpu.VMEM((hps, 16, c), F32)],
        compiler_params=_cparams(("arbitrary", "arbitrary")),
        name="gated_deltanet",
    )(proj_main, proj_main, proj_main, proj_main, tail, arow, conv_w, conv_w, conv_w,
      a_log_b, dt_bias_b, g_gnorm)


OUT_TM = 512
OUT_TK = 512
EPI_ROWS = 256


def _dot_f32(a, b_parts):
    a_hi, a_mid, a_lo = _split3(a)
    b_hi, b_mid, b_lo = b_parts
    acc = _dot(a_hi, b_hi)
    acc += _dot(a_hi, b_mid) + _dot(a_mid, b_hi)
    acc += _dot(a_hi, b_lo) + _dot(a_mid, b_mid) + _dot(a_lo, b_hi)
    return acc


def _outproj_kernel(od_ref, on_ref, w_ref, x_ref, g1_ref, gf_ref, sc_ref, sh_ref, wr_ref, br_ref,
                    x1_ref, hn2_ref, idx_ref, gate_ref, rank_ref, cnt_ref, acc_ref, mh_ref, carry_ref):
    i = pl.program_id(0)
    k = pl.program_id(1)
    nk = pl.num_programs(1)
    tm = OUT_TM

    @pl.when((i == 0) & (k == 0))
    def _():
        carry_ref[...] = jnp.zeros_like(carry_ref)

    lhs = jnp.where(k < nk // 2, od_ref[...], on_ref[...])
    part = _dot(lhs, w_ref[...])

    @pl.when(k == 0)
    def _():
        acc_ref[...] = part

    @pl.when(k > 0)
    def _():
        acc_ref[...] += part

    @pl.when(k == nk - 1)
    def _():
        g1 = g1_ref[0]
        gf = gf_ref[...]
        sc = sc_ref[0]
        sh = sh_ref[0]
        wr_parts = _split3(wr_ref[...])
        br = br_ref[...]
        lane = lax.broadcasted_iota(jnp.int32, (EPI_ROWS, HEAD_LANES), 1)
        for r in range(0, tm, EPI_ROWS):
            rows = slice(r, r + EPI_ROWS)
            x1 = x_ref[rows, :] + g1 * acc_ref[rows, :]
            x1_ref[rows, :] = x1
            hn2 = _norm_modulate(x1, gf, sc, sh)
            hn2_ref[rows, :] = hn2
            logits = _dot_f32(hn2, wr_parts) + br
            l = jnp.where(lane < N_EXPERTS, logits, -jnp.inf)
            vals, idxs = [], []
            for _ in range(TOP_K):
                m = jnp.max(l, axis=-1, keepdims=True)
                ix = jnp.min(jnp.where(l == m, lane, HEAD_LANES), axis=-1, keepdims=True)
                vals.append(m)
                idxs.append(ix)
                l = jnp.where(lane == ix, -jnp.inf, l)
            es = [jnp.exp(v - vals[0]) for v in vals]
            den = (es[0] + es[1]) + (es[2] + es[3])
            idx_o = jnp.zeros((EPI_ROWS, HEAD_LANES), jnp.int32)
            gate_o = jnp.zeros((EPI_ROWS, HEAD_LANES), F32)
            hot = jnp.zeros((EPI_ROWS, HEAD_LANES), F32)
            for kk in range(TOP_K):
                idx_o = jnp.where(lane == kk, idxs[kk], idx_o)
                gate_o = jnp.where(lane == kk, es[kk] / den, gate_o)
                hot = jnp.where(lane == idxs[kk], 1.0, hot)
            idx_ref[rows, :] = idx_o
            gate_ref[rows, :] = gate_o
            mh_ref[rows, :] = hot.astype(BF16)

        ri = lax.broadcasted_iota(jnp.int32, (tm, tm), 0)
        ci = lax.broadcasted_iota(jnp.int32, (tm, tm), 1)
        mh = mh_ref[...]
        carry = carry_ref[0:1, :]
        cum = _dot(_ones_where(ci < ri), mh) + carry
        lane_t = lax.broadcasted_iota(jnp.int32, (tm, HEAD_LANES), 1)
        idx_t = idx_ref[...]
        rank_o = jnp.zeros((tm, HEAD_LANES), F32)
        for kk in range(TOP_K):
            ik = idx_t[:, kk:kk + 1]
            rk = jnp.sum(jnp.where(lane_t == ik, cum, 0.0), axis=-1, keepdims=True)
            rank_o = jnp.where(lane_t == kk, rk, rank_o)
        rank_ref[...] = rank_o.astype(jnp.int32)
        total = carry + jnp.sum(mh.astype(F32), axis=0, keepdims=True)
        carry_ref[...] = jnp.broadcast_to(total, carry_ref.shape)
        cnt_ref[...] = jnp.broadcast_to(total, cnt_ref.shape).astype(jnp.int32)


def _outproj_router(o_diff, o_dn, w_out, x2, gate1, g_ffn, scale2, shift2, w_router_pad, b_router_pad, seq):
    n_tok, d = x2.shape
    tm, tk = OUT_TM, OUT_TK
    tiles_per_batch = seq // tm
    nk = d // tk
    half = nk // 2
    vec3 = pl.BlockSpec((1, 1, d), lambda i, k: (i // tiles_per_batch, 0, 0))
    row_out = lambda dt: jax.ShapeDtypeStruct((n_tok, HEAD_LANES), dt)
    return pl.pallas_call(
        _outproj_kernel,
        grid=(n_tok // tm, nk),
        in_specs=[
            pl.BlockSpec((tm, tk), lambda i, k: (i, jnp.minimum(k, half - 1))),
            pl.BlockSpec((tm, tk), lambda i, k: (i, jnp.maximum(k - half, 0))),
            pl.BlockSpec((tk, d), lambda i, k: (k, 0)),
            pl.BlockSpec((tm, d), lambda i, k: (i, 0)),
            vec3,
            pl.BlockSpec((1, d), lambda i, k: (0, 0)),
            vec3, vec3,
            pl.BlockSpec((d, HEAD_LANES), lambda i, k: (0, 0)),
            pl.BlockSpec((1, HEAD_LANES), lambda i, k: (0, 0)),
        ],
        out_specs=[
            pl.BlockSpec((tm, d), lambda i, k: (i, 0)),
            pl.BlockSpec((tm, d), lambda i, k: (i, 0)),
            pl.BlockSpec((tm, HEAD_LANES), lambda i, k: (i, 0)),
            pl.BlockSpec((tm, HEAD_LANES), lambda i, k: (i, 0)),
            pl.BlockSpec((tm, HEAD_LANES), lambda i, k: (i, 0)),
            pl.BlockSpec((8, HEAD_LANES), lambda i, k: (0, 0)),
        ],
        out_shape=[
            jax.ShapeDtypeStruct((n_tok, d), F32),
            jax.ShapeDtypeStruct((n_tok, d), F32),
            row_out(jnp.int32), row_out(F32), row_out(jnp.int32),
            jax.ShapeDtypeStruct((8, HEAD_LANES), jnp.int32),
        ],
        scratch_shapes=[pltpu.VMEM((tm, d), F32), pltpu.VMEM((tm, HEAD_LANES), BF16),
                        pltpu.VMEM((8, HEAD_LANES), F32)],
        compiler_params=_cparams(("arbitrary", "arbitrary")),
        name="outproj_router",
    )(o_diff, o_dn, w_out, x2, gate1, g_ffn, scale2, shift2, w_router_pad, b_router_pad)


MOE_TM = 1024
MOE_SUB = 256
MOE_TF = 512
MOE_TN = 512
DSP_TM = 256


def _row_copy(src_hbm, row, dst_vmem, slot, sem):
    return pltpu.make_async_copy(src_hbm.at[pl.ds(row, 1)], dst_vmem.at[pl.ds(slot, 1)], sem)


def _dispatch_kernel(sbr_ref, sbm_ref, dest_ref, hn2_ref, xs_hbm, zero_ref, sem):
    i = pl.program_id(0)
    tm = DSP_TM

    @pl.when(i == 0)
    def _():
        zero_ref[...] = jnp.zeros_like(zero_ref)
        n_sb = sbr_ref.shape[0]

        def pad_copies(s, fn):
            nv = sbr_ref[s]
            nv8 = (nv + 7) // 8 * 8
            for t in range(7):
                @pl.when(nv + t < nv8)
                def _():
                    fn(pltpu.make_async_copy(zero_ref.at[pl.ds(0, 1)],
                                             xs_hbm.at[pl.ds(s * MOE_TM + nv + t, 1)], sem))

            pad8 = (sbm_ref[s] - nv8) // 8
            bit = MOE_SUB // 16
            while bit >= 1:
                rows = 8 * bit
                start = pl.multiple_of(s * MOE_TM + nv8 + 8 * (pad8 & ~(2 * bit - 1)), 8)

                @pl.when((pad8 & bit) != 0)
                def _():
                    fn(pltpu.make_async_copy(zero_ref.at[pl.ds(0, rows)],
                                             xs_hbm.at[pl.ds(start, rows)], sem))

                bit //= 2
            return 0

        lax.fori_loop(0, n_sb, lambda s, c: c + pad_copies(s, lambda cp: cp.start()), 0)
        lax.fori_loop(0, n_sb, lambda s, c: c + pad_copies(s, lambda cp: cp.wait()), 0)

    def issue(r, carry):
        for kk in range(TOP_K):
            pltpu.make_async_copy(hn2_ref.at[pl.ds(r, 1)],
                                  xs_hbm.at[pl.ds(dest_ref[0, 0, r * TOP_K + kk], 1)], sem).start()
        return carry

    lax.fori_loop(0, tm, issue, 0)

    def wait(r, carry):
        for kk in range(TOP_K):
            pltpu.make_async_copy(hn2_ref.at[pl.ds(0, 1)], xs_hbm.at[pl.ds(0, 1)], sem).wait()
        return carry

    lax.fori_loop(0, tm, wait, 0)


def _dispatch(sb_rows, sb_m, dest_tiles, hn2, n_rows):
    n_tok, d = hn2.shape
    tm = DSP_TM
    grid_spec = pltpu.PrefetchScalarGridSpec(
        num_scalar_prefetch=2,
        grid=(n_tok // tm,),
        in_specs=[
            pl.BlockSpec((1, 1, tm * TOP_K), lambda i, sbr, sbm: (i, 0, 0), memory_space=pltpu.SMEM),
            pl.BlockSpec((tm, d), lambda i, sbr, sbm: (i, 0)),
        ],
        out_specs=pl.BlockSpec(memory_space=pl.ANY),
        scratch_shapes=[pltpu.VMEM((MOE_SUB // 2, d), F32), pltpu.SemaphoreType.DMA(())],
    )
    return pl.pallas_call(
        _dispatch_kernel,
        grid_spec=grid_spec,
        out_shape=jax.ShapeDtypeStruct((n_rows, d), F32),
        compiler_params=_cparams(("arbitrary",)),
        name="moe_dispatch",
    )(sb_rows, sb_m, dest_tiles, hn2)


def _moe_up_kernel(sbe_ref, sbm_ref, nused_ref, x_ref, wg_ref, wu_ref, bg_ref, bu_ref, h_ref, xb_ref):
    s = pl.program_id(0)
    f = pl.program_id(1)
    m = sbm_ref[s]

    for r in range(0, MOE_TM, MOE_SUB):
        @pl.when((f == 0) & (r < m))
        def _():
            xb_ref[r:r + MOE_SUB, :] = x_ref[r:r + MOE_SUB, :].astype(BF16)

    for rows in range(MOE_SUB, MOE_TM + 1, MOE_SUB):
        @pl.when(m == rows)
        def _():
            wg = wg_ref[0].astype(BF16)
            wu = wu_ref[0].astype(BF16)
            xq = xb_ref[0:rows, :]
            gate = jnp.minimum(_dot(xq, wg) + bg_ref[0], SWIGLU_LIMIT)
            up = jnp.clip(_dot(xq, wu) + bu_ref[0], -SWIGLU_LIMIT, SWIGLU_LIMIT)
            glu = gate * jax.nn.sigmoid(gate * SWIGLU_ALPHA)
            h_ref[0:rows, :] = ((up + 1.0) * glu).astype(BF16)
            if rows < MOE_TM:
                h_ref[rows:MOE_TM, :] = jnp.zeros((MOE_TM - rows, h_ref.shape[1]), BF16)


def _moe_up(sb_expert, sb_m, n_used, xs, w_gate_up, b_gate_up):
    n_sb = sb_expert.shape[0]
    d = xs.shape[1]
    n_exp, _, f2 = w_gate_up.shape
    f_dim = f2 // 2
    nf = f_dim // MOE_TF

    def f_eff(s, f, nu):
        return jnp.where(s < nu[0], f, nf - 1)

    def s_eff(s, nu):
        return jnp.minimum(s, nu[0] - 1)

    grid_spec = pltpu.PrefetchScalarGridSpec(
        num_scalar_prefetch=3,
        grid=(n_sb, nf),
        in_specs=[
            pl.BlockSpec((MOE_TM, d), lambda s, f, sbe, sbm, nu: (s_eff(s, nu), 0)),
            pl.BlockSpec((1, d, MOE_TF), lambda s, f, sbe, sbm, nu: (sbe[s], 0, f_eff(s, f, nu))),
            pl.BlockSpec((1, d, MOE_TF), lambda s, f, sbe, sbm, nu: (sbe[s], 0, nf + f_eff(s, f, nu))),
            pl.BlockSpec((1, 1, MOE_TF), lambda s, f, sbe, sbm, nu: (sbe[s], 0, f_eff(s, f, nu))),
            pl.BlockSpec((1, 1, MOE_TF), lambda s, f, sbe, sbm, nu: (sbe[s], 0, nf + f_eff(s, f, nu))),
        ],
        out_specs=pl.BlockSpec((MOE_TM, MOE_TF), lambda s, f, sbe, sbm, nu: (s_eff(s, nu), f_eff(s, f, nu))),
        scratch_shapes=[pltpu.VMEM((MOE_TM, d), BF16)],
    )
    b3 = b_gate_up.reshape(n_exp, 1, f2)
    return pl.pallas_call(
        _moe_up_kernel,
        grid_spec=grid_spec,
        out_shape=jax.ShapeDtypeStruct((n_sb * MOE_TM, f_dim), BF16),
        compiler_params=_cparams(("arbitrary", "arbitrary")),
        name="moe_up",
    )(sb_expert, sb_m, n_used, xs, w_gate_up, w_gate_up, b3, b3)


def _moe_down_kernel(sbe_ref, sbm_ref, nused_ref, h_ref, wd_ref, bd_ref, o_ref):
    s = pl.program_id(0)
    m = sbm_ref[s]

    for rows in range(MOE_SUB, MOE_TM + 1, MOE_SUB):
        @pl.when(m == rows)
        def _():
            o_ref[0:rows, :] = _dot(h_ref[0:rows, :], wd_ref[0].astype(BF16)) + bd_ref[0]
            if rows < MOE_TM:
                o_ref[rows:MOE_TM, :] = jnp.zeros((MOE_TM - rows, o_ref.shape[1]), F32)


def _moe_down(sb_expert, sb_m, n_used, h, w_down, b_down):
    n_sb = sb_expert.shape[0]
    n_exp, f_dim, d = w_down.shape
    nn = d // MOE_TN

    def n_eff(s, n, nu):
        return jnp.where(s < nu[0], n, nn - 1)

    def s_eff(s, nu):
        return jnp.minimum(s, nu[0] - 1)

    grid_spec = pltpu.PrefetchScalarGridSpec(
        num_scalar_prefetch=3,
        grid=(n_sb, nn),
        in_specs=[
            pl.BlockSpec((MOE_TM, f_dim), lambda s, n, sbe, sbm, nu: (s_eff(s, nu), 0)),
            pl.BlockSpec((1, f_dim, MOE_TN), lambda s, n, sbe, sbm, nu: (sbe[s], 0, n_eff(s, n, nu))),
            pl.BlockSpec((1, 1, MOE_TN), lambda s, n, sbe, sbm, nu: (sbe[s], 0, n_eff(s, n, nu))),
        ],
        out_specs=pl.BlockSpec((MOE_TM, MOE_TN), lambda s, n, sbe, sbm, nu: (s_eff(s, nu), n_eff(s, n, nu))),
    )
    return pl.pallas_call(
        _moe_down_kernel,
        grid_spec=grid_spec,
        out_shape=jax.ShapeDtypeStruct((n_sb * MOE_TM, d), F32),
        compiler_params=_cparams(("arbitrary", "arbitrary")),
        name="moe_down",
    )(sb_expert, sb_m, n_used, h, w_down, b_down.reshape(n_exp, 1, d))


CMB_TM = 256


def _combine_kernel(dest_ref, ys_hbm, x1_ref, gate_ref, g2_ref, gfin_ref, o_ref, buf_ref, sem):
    tm = CMB_TM

    def issue(r, carry):
        for kk in range(TOP_K):
            _row_copy(ys_hbm, dest_ref[0, 0, r * TOP_K + kk], buf_ref.at[kk], r, sem).start()
        return carry

    lax.fori_loop(0, tm, issue, 0)

    def wait(r, carry):
        for kk in range(TOP_K):
            _row_copy(ys_hbm, 0, buf_ref.at[kk], 0, sem).wait()
        return carry

    lax.fori_loop(0, tm, wait, 0)

    gates = gate_ref[...]
    y = gates[:, 0:1] * buf_ref[0]
    for kk in range(1, TOP_K):
        y = y + gates[:, kk:kk + 1] * buf_ref[kk]
    x2 = x1_ref[...] + g2_ref[0] * y
    ms = jnp.mean(x2 * x2, axis=-1, keepdims=True)
    o_ref[...] = (x2 * lax.rsqrt(ms + EPS)) * gfin_ref[...]


def _combine(dest_tiles, ys, x1, gates, gate2, g_final, seq):
    n_tok, d = x1.shape
    tm = CMB_TM
    tiles_per_batch = seq // tm
    return pl.pallas_call(
        _combine_kernel,
        grid=(n_tok // tm,),
        in_specs=[
            pl.BlockSpec((1, 1, tm * TOP_K), lambda i: (i, 0, 0), memory_space=pltpu.SMEM),
            pl.BlockSpec(memory_space=pl.ANY),
            pl.BlockSpec((tm, d), lambda i: (i, 0)),
            pl.BlockSpec((tm, HEAD_LANES), lambda i: (i, 0)),
            pl.BlockSpec((1, 1, d), lambda i: (i // tiles_per_batch, 0, 0)),
            pl.BlockSpec((1, d), lambda i: (0, 0)),
        ],
        out_specs=pl.BlockSpec((tm, d), lambda i: (i, 0)),
        out_shape=jax.ShapeDtypeStruct((n_tok, d), F32),
        scratch_shapes=[pltpu.VMEM((TOP_K, tm, d), F32), pltpu.SemaphoreType.DMA(())],
        compiler_params=_cparams(("arbitrary",)),
        name="moe_combine_norm",
    )(dest_tiles, ys, x1, gates, gate2, g_final)


def kernel(x, c, w_ada, b_ada, g_norm_mix, g_norm_ffn, g_norm_final, w_in, conv_w, rel_bias,
           lambda_q1, lambda_k1, lambda_q2, lambda_k2, g_subln, a_log, dt_bias, g_gnorm, w_out,
           w_router, b_router, w_gate_up, b_gate_up, w_down, b_down):
    batch, seq, d = x.shape
    n_tok = batch * seq
    depth = w_ada.shape[0]
    assert depth == 1 and d == D_MODEL
    x2 = x.reshape(n_tok, d)
    bias_tiles = _bias_tiles(rel_bias)
    for l in range(depth):
        c_pad = jnp.zeros((16, d), F32).at[:batch].set(c)
        mod = _adaln(c_pad, w_ada[l], b_ada[l][None, :])[:batch]
        shift1, scale1, gate1, shift2, scale2, gate2 = [m.reshape(batch, 1, d) for m in jnp.split(mod, 6, axis=-1)]

        w_tail = jnp.pad(w_in[l][:, MAIN_COLS:], ((0, 0), (0, HEAD_LANES - 2 * DN_HEADS)))
        w_tail_t = w_in[l][:, MAIN_COLS:].T
        proj_main, tail, tail_t = _inproj(x2, g_norm_mix[l][None, :], scale1, shift1, _to_bf16(w_in[l], MAIN_COLS), w_tail, w_tail_t,
                                          batch, seq)
        o_diff = _diff_attention(proj_main, bias_tiles, lambda_q1[l][None, :], lambda_k1[l][None, :],
                                 lambda_q2[l][None, :], lambda_k2[l][None, :], g_subln[l][None, :], batch, seq)
        a_log_b = jnp.broadcast_to(a_log[l][:, None, None], (DN_HEADS, 1, HEAD_LANES))
        dt_bias_b = jnp.broadcast_to(dt_bias[l][:, None, None], (DN_HEADS, 1, HEAD_LANES))
        o_dn = _gated_deltanet(proj_main, tail, tail_t, conv_w[l], a_log_b, dt_bias_b, g_gnorm[l][None, :],
                               batch, seq)

        w_router_pad = jnp.pad(w_router[l], ((0, 0), (0, HEAD_LANES - N_EXPERTS)))
        b_router_pad = jnp.pad(b_router[l], (0, HEAD_LANES - N_EXPERTS))[None, :]
        x1, hn2, top_idx, gates, rank, counts = _outproj_router(
            o_diff.reshape(n_tok, -1), o_dn.reshape(n_tok, -1), _to_bf16(w_out[l], d), x2, gate1, g_norm_ffn[l][None, :],
            scale2, shift2, w_router_pad, b_router_pad, seq)

        n_assign = n_tok * TOP_K
        n_sb = n_assign // MOE_TM + N_EXPERTS
        cnt = counts[0, :N_EXPERTS]
        sb_per_e = (cnt + MOE_TM - 1) // MOE_TM
        rows_per_sb = (cnt + jnp.maximum(sb_per_e, 1) - 1) // jnp.maximum(sb_per_e, 1)
        sb_end = jnp.cumsum(sb_per_e)
        sb_start = sb_end - sb_per_e
        n_used = sb_end[-1]
        s_ids = jnp.arange(n_sb, dtype=jnp.int32)
        e_ids = jnp.arange(N_EXPERTS, dtype=jnp.int32)
        s_clamped = jnp.minimum(s_ids, n_used - 1)
        e_of_s = jnp.sum((sb_end[None, :] <= s_clamped[:, None]).astype(jnp.int32), axis=1)
        e_of_s = jnp.minimum(e_of_s, N_EXPERTS - 1).astype(jnp.int32)
        onehot_s = (e_of_s[:, None] == e_ids[None, :]).astype(jnp.int32)
        j_of_s = s_ids - jnp.sum(onehot_s * sb_start[None, :], axis=1)
        per_s = jnp.sum(onehot_s * rows_per_sb[None, :], axis=1)
        cnt_s = jnp.sum(onehot_s * cnt[None, :], axis=1)
        sb_rows = jnp.where(s_ids < n_used, jnp.clip(cnt_s - j_of_s * per_s, 0, per_s), 0).astype(jnp.int32)
        sb_m = ((sb_rows + MOE_SUB - 1) // MOE_SUB * MOE_SUB).astype(jnp.int32)
        idx4 = top_idx[:, :TOP_K]
        onehot_a = (idx4[:, :, None] == e_ids[None, None, :]).astype(jnp.int32)
        start_a = jnp.sum(onehot_a * sb_start[None, None, :], axis=-1)
        per_a = jnp.maximum(jnp.sum(onehot_a * rows_per_sb[None, None, :], axis=-1), 1)
        rank4 = rank[:, :TOP_K]
        dest = ((start_a + rank4 // per_a) * MOE_TM + rank4 % per_a).astype(jnp.int32)
        n_used1 = n_used.reshape(1).astype(jnp.int32)

        xs = _dispatch(sb_rows, sb_m, dest.reshape(n_tok // DSP_TM, 1, DSP_TM * TOP_K), hn2, n_sb * MOE_TM)
        hmid = _moe_up(e_of_s, sb_m, n_used1, xs, w_gate_up[l], b_gate_up[l])
        ys = _moe_down(e_of_s, sb_m, n_used1, hmid, w_down[l], b_down[l])
        x2 = _combine(dest.reshape(n_tok // CMB_TM, 1, CMB_TM * TOP_K), ys, x1, gates, gate2,
                      g_norm_final[None, :], seq)
    return x2.reshape(batch, seq, d)
```

```python
import functools
import math

import jax
import jax.numpy as jnp
import numpy as np
from jax import lax
from jax.experimental import pallas as pl
from jax.experimental.pallas import tpu as pltpu

F32 = jnp.float32
BF16 = jnp.bfloat16

D_MODEL = 2048
DIFF_HEADS = 8
DIFF_HEAD_DIM = 64
DIFF_V_DIM = 128
DN_HEADS = 8
DN_HEAD_DIM = 128
HEAD_LANES = 128
CONV_K = 4
N_BUCKETS = 32
MAX_DISTANCE = 128
N_EXPERTS = 32
TOP_K = 4
D_EXPERT = 2048
SWIGLU_LIMIT = 7.0
SWIGLU_ALPHA = 1.702
EPS = 1e-6
LAMBDA_INIT = 0.8 - 0.6 * math.exp(-0.3 * 0)
N_MAIN_GROUPS = 7
MAIN_COLS = N_MAIN_GROUPS * 1024

V7X_VMEM_BYTES = 64 * 1024 * 1024
VMEM_LIMIT = 56 * 1024 * 1024

NEG_BIG = -1e30


def _cparams(sem):
    return pltpu.CompilerParams(dimension_semantics=sem, vmem_limit_bytes=VMEM_LIMIT)


def _split3(a):
    hi = a.astype(BF16)
    r1 = a - hi.astype(F32)
    mid = r1.astype(BF16)
    lo = (r1 - mid.astype(F32)).astype(BF16)
    return hi, mid, lo


def _dot(a, b, dims=None):
    if dims is None:
        return jnp.dot(a, b, preferred_element_type=F32)
    return lax.dot_general(a, b, (dims, ((), ())), preferred_element_type=F32)


def _dot_exact_lhs(a, b_bf16, dims=None):
    hi, mid, lo = _split3(a)
    return (_dot(hi, b_bf16, dims) + _dot(mid, b_bf16, dims)) + _dot(lo, b_bf16, dims)


def _dot_exact_rhs(a_bf16, b, dims=None):
    hi, mid, lo = _split3(b)
    return (_dot(a_bf16, hi, dims) + _dot(a_bf16, mid, dims)) + _dot(a_bf16, lo, dims)


NT = ((1,), (1,))


def _ones_where(mask):
    return jnp.where(mask, 1.0, 0.0).astype(BF16)


ADA_TN = 1024


def _adaln_kernel(c_ref, w_ref, b_ref, o_ref):
    c = c_ref[...]
    sc = c * jax.nn.sigmoid(c)
    a_hi, a_mid, a_lo = _split3(sc)
    w_hi, w_mid, w_lo = _split3(w_ref[...])
    acc = _dot(a_hi, w_hi)
    acc += _dot(a_hi, w_mid) + _dot(a_mid, w_hi)
    acc += _dot(a_hi, w_lo) + _dot(a_mid, w_mid) + _dot(a_lo, w_hi)
    o_ref[...] = acc + b_ref[...]


def _adaln(c_pad, w, b):
    n = w.shape[1]
    return pl.pallas_call(
        _adaln_kernel,
        grid=(n // ADA_TN,),
        in_specs=[
            pl.BlockSpec(c_pad.shape, lambda j: (0, 0)),
            pl.BlockSpec((w.shape[0], ADA_TN), lambda j: (0, j)),
            pl.BlockSpec((1, ADA_TN), lambda j: (0, j)),
        ],
        out_specs=pl.BlockSpec((c_pad.shape[0], ADA_TN), lambda j: (0, j)),
        out_shape=jax.ShapeDtypeStruct((c_pad.shape[0], n), F32),
        compiler_params=_cparams(("arbitrary",)),
        name="adaln_mod",
    )(c_pad, w, b)


CAST_ROWS = 256


def _cast_kernel(w_ref, o_ref):
    o_ref[...] = w_ref[0].astype(BF16)


def _to_bf16(w, layer, n_cols):
    k = w.shape[1]
    return pl.pallas_call(
        _cast_kernel,
        grid=(k // CAST_ROWS,),
        in_specs=[pl.BlockSpec((1, CAST_ROWS, n_cols), lambda i: (layer, i, 0))],
        out_specs=pl.BlockSpec((CAST_ROWS, n_cols), lambda i: (i, 0)),
        out_shape=jax.ShapeDtypeStruct((k, n_cols), BF16),
        compiler_params=_cparams(("arbitrary",)),
        name="weights_to_bf16",
    )(w)


INPROJ_TM = 1024
INPROJ_TN = 1024
NORM_ROWS = 256


def _norm_modulate(x, g, scale, shift):
    ms = jnp.mean(x * x, axis=-1, keepdims=True)
    y = (x * lax.rsqrt(ms + EPS)) * g
    return y * (1.0 + scale) + shift


def _inproj_kernel(x_ref, g_ref, sc_ref, sh_ref, w_ref, wt_ref, wtt_ref, o_ref, t_ref, tt_ref, hn_ref):
    j = pl.program_id(1)

    @pl.when(j == 0)
    def _():
        g = g_ref[...]
        sc = sc_ref[0]
        sh = sh_ref[0]
        for r in range(0, INPROJ_TM, NORM_ROWS):
            hn = _norm_modulate(x_ref[r:r + NORM_ROWS, :], g, sc, sh)
            hn_ref[r:r + NORM_ROWS, :] = hn.astype(BF16)
        hnb = hn_ref[...]
        t_ref[...] = _dot(hnb, wt_ref[...].astype(BF16))
        tt_ref[...] = _dot(wtt_ref[...].astype(BF16), hnb, NT)

    acc = _dot(hn_ref[...], w_ref[...])
    for hh in range(INPROJ_TN // HEAD_LANES):
        o_ref[0, 0, hh] = acc[:, hh * HEAD_LANES:(hh + 1) * HEAD_LANES]


def _inproj(x2, g_mix, scale1, shift1, w_in, w_tail, w_tail_t, batch, seq):
    n_tok, d = x2.shape
    tiles_per_batch = seq // INPROJ_TM
    heads_per_tile = INPROJ_TN // HEAD_LANES
    tiles_per_group = 1024 // INPROJ_TN
    grid = (n_tok // INPROJ_TM, MAIN_COLS // INPROJ_TN)
    return pl.pallas_call(
        _inproj_kernel,
        grid=grid,
        in_specs=[
            pl.BlockSpec((INPROJ_TM, d), lambda i, j: (i, 0)),
            pl.BlockSpec((1, d), lambda i, j: (0, 0)),
            pl.BlockSpec((1, 1, d), lambda i, j: (i // tiles_per_batch, 0, 0)),
            pl.BlockSpec((1, 1, d), lambda i, j: (i // tiles_per_batch, 0, 0)),
            pl.BlockSpec((d, INPROJ_TN), lambda i, j: (0, j)),
            pl.BlockSpec((d, HEAD_LANES), lambda i, j: (0, 0)),
            pl.BlockSpec((16, d), lambda i, j: (0, 0)),
        ],
        out_specs=[
            pl.BlockSpec((1, 1, heads_per_tile, INPROJ_TM, HEAD_LANES),
                         lambda i, j: (j // tiles_per_group, i // tiles_per_batch, j % tiles_per_group,
                                       i % tiles_per_batch, 0)),
            pl.BlockSpec((INPROJ_TM, HEAD_LANES), lambda i, j: (i, 0)),
            pl.BlockSpec((16, INPROJ_TM), lambda i, j: (0, i)),
        ],
        out_shape=[
            jax.ShapeDtypeStruct((N_MAIN_GROUPS, batch, 8, seq, HEAD_LANES), F32),
            jax.ShapeDtypeStruct((n_tok, HEAD_LANES), F32),
            jax.ShapeDtypeStruct((16, n_tok), F32),
        ],
        scratch_shapes=[pltpu.VMEM((INPROJ_TM, d), BF16)],
        compiler_params=_cparams(("arbitrary", "arbitrary")),
        name="norm_inproj",
    )(x2, g_mix, scale1, shift1, w_in, w_tail, w_tail_t)


ATT_T = 256
ATT_ROWS = 128


def _t5_bucket(n):
    n = jnp.maximum(n, 0)
    max_exact = N_BUCKETS // 2
    nf = jnp.maximum(n, max_exact).astype(F32)
    large = max_exact + (jnp.log(nf / max_exact) / math.log(MAX_DISTANCE / max_exact)
                         * (N_BUCKETS - max_exact)).astype(jnp.int32)
    large = jnp.minimum(large, N_BUCKETS - 1)
    return jnp.where(n < max_exact, n, large)


def _bias_tiles(rel_bias):
    assert ATT_T >= MAX_DISTANCE
    t = ATT_T
    table = rel_bias.astype(F32)
    n_heads = table.shape[1]
    tiles = []
    for d in range(2):
        dist = (d * t + t) - jnp.arange(2 * t, dtype=jnp.int32)
        w = jnp.where(dist[:, None] >= 0, table[_t5_bucket(dist)], F32(NEG_BIG)).T
        w_ext = jnp.concatenate([w, jnp.zeros((n_heads, 1), F32)], axis=1)
        skew = jnp.tile(w_ext, (1, t))[:, :t * 2 * t].reshape(n_heads, t, 2 * t)
        tiles.append(skew[:, :, t:])
    far = jnp.broadcast_to(table[N_BUCKETS - 1][:, None, None], (n_heads, t, t))
    return jnp.stack(tiles + [far], axis=1)


def _attn_kernel(q_ref, k_ref, v_ref, bias_ref, lq1_ref, lk1_ref, lq2_ref, lk2_ref, gs_ref,
                 o_ref, kb_ref, vb_ref, s0_ref, s1_ref, p_ref, m_ref, l_ref, acc_ref):
    i = pl.program_id(2)
    t = ATT_T
    rows2 = 2 * t

    @pl.when(i == 0)
    def _():
        kb_ref[...] = k_ref[0, 0, 0].astype(BF16)
        vb_ref[...] = v_ref[0, 0, 0].astype(BF16)

    q = q_ref[0, 0, 0] * (DIFF_HEAD_DIM ** -0.5)
    lane = lax.broadcasted_iota(jnp.int32, q.shape, 1)
    q1 = jnp.where(lane < DIFF_HEAD_DIM, q, 0.0)
    q2 = jnp.where(lane >= DIFF_HEAD_DIM, q, 0.0)
    q12 = jnp.concatenate([q1, q2], axis=0).astype(BF16)

    m_ref[...] = jnp.full(m_ref.shape, -jnp.inf, F32)
    l_ref[...] = jnp.zeros(l_ref.shape, F32)
    acc_ref[...] = jnp.zeros(acc_ref.shape, F32)

    def scores(j):
        return _dot(q12, kb_ref[pl.ds(pl.multiple_of(j * t, t), t), :], NT)

    def block(j, s_cur_ref, s_next_ref):
        s_next_ref[...] = scores(jnp.minimum(j + 1, i))
        d = jnp.minimum(i - j, 2)
        for r in range(0, rows2, ATT_ROWS):
            rows = slice(r, r + ATT_ROWS)
            brows = slice(r % t, r % t + ATT_ROWS)
            halves = [s_cur_ref[rows, c:c + HEAD_LANES] + bias_ref[0, d, brows, c:c + HEAD_LANES]
                      for c in range(0, t, HEAD_LANES)]
            m_old = m_ref[rows, :]
            m_new = jnp.maximum(m_old, jnp.max(functools.reduce(jnp.maximum, halves), axis=-1, keepdims=True))
            alpha = jnp.exp(m_old - m_new)
            ps = [jnp.exp(sh - m_new) for sh in halves]
            l_ref[rows, :] = alpha * l_ref[rows, :] + jnp.sum(functools.reduce(jnp.add, ps), axis=-1, keepdims=True)
            m_ref[rows, :] = m_new
            for c, ph in zip(range(0, t, HEAD_LANES), ps):
                p_ref[rows, c:c + HEAD_LANES] = ph.astype(BF16)
            acc_ref[rows, :] = alpha * acc_ref[rows, :]
        acc_ref[...] += _dot(p_ref[...], vb_ref[pl.ds(pl.multiple_of(j * t, t), t), :])

    s0_ref[...] = scores(0)

    def pair(pidx, carry):
        j0 = 2 * pidx
        block(j0, s0_ref, s1_ref)

        @pl.when(j0 + 1 <= i)
        def _():
            block(j0 + 1, s1_ref, s0_ref)

        return carry

    lax.fori_loop(0, (i + 2) // 2, pair, 0)

    lam = (jnp.exp(jnp.sum(lq1_ref[...] * lk1_ref[...], axis=-1, keepdims=True))
           - jnp.exp(jnp.sum(lq2_ref[...] * lk2_ref[...], axis=-1, keepdims=True)) + LAMBDA_INIT)
    o = acc_ref[0:t, :] / l_ref[0:t, :] - lam * (acc_ref[t:rows2, :] / l_ref[t:rows2, :])
    ms = jnp.mean(o * o, axis=-1, keepdims=True)
    y = (o * lax.rsqrt(ms + EPS)) * gs_ref[...]
    o_ref[0] = (y * (1.0 - LAMBDA_INIT)).astype(o_ref.dtype)


def _diff_attention(proj_main, bias_tiles, lq1, lk1, lq2, lk2, g_subln, batch, seq):
    t = ATT_T
    nq = seq // t
    lam_spec = pl.BlockSpec((1, DIFF_HEAD_DIM), lambda b, h, i: (0, 0))
    return pl.pallas_call(
        _attn_kernel,
        grid=(batch, DIFF_HEADS, nq),
        in_specs=[
            pl.BlockSpec((1, 1, 1, t, HEAD_LANES), lambda b, h, i: (0, b, h, i, 0)),
            pl.BlockSpec((1, 1, 1, seq, HEAD_LANES), lambda b, h, i: (1, b, h, 0, 0)),
            pl.BlockSpec((1, 1, 1, seq, HEAD_LANES), lambda b, h, i: (2, b, h, 0, 0)),
            pl.BlockSpec((1, 3, t, t), lambda b, h, i: (h, 0, 0, 0)),
            lam_spec, lam_spec, lam_spec, lam_spec,
            pl.BlockSpec((1, DIFF_V_DIM), lambda b, h, i: (0, 0)),
        ],
        out_specs=pl.BlockSpec((1, t, HEAD_LANES), lambda b, h, i: (b, i, h)),
        scratch_shapes=[
            pltpu.VMEM((seq, HEAD_LANES), BF16), pltpu.VMEM((seq, HEAD_LANES), BF16),
            pltpu.VMEM((2 * t, t), F32), pltpu.VMEM((2 * t, t), F32), pltpu.VMEM((2 * t, t), BF16),
            pltpu.VMEM((2 * t, HEAD_LANES), F32), pltpu.VMEM((2 * t, HEAD_LANES), F32),
            pltpu.VMEM((2 * t, DIFF_V_DIM), F32),
        ],
        out_shape=jax.ShapeDtypeStruct((batch, seq, DIFF_HEADS * DIFF_V_DIM), BF16),
        compiler_params=_cparams(("arbitrary", "arbitrary", "arbitrary")),
        name="diff_attention",
    )(proj_main, proj_main, proj_main, bias_tiles, lq1, lk1, lq2, lk2, g_subln)


GDN_C = 256
GDN_BASE = 16
GDN_HEADS_PER_STEP = 2


def _softplus(x):
    return jnp.maximum(x, 0.0) + jnp.log1p(jnp.exp(-jnp.abs(x)))


def _silu(x):
    return x * jax.nn.sigmoid(x)


def _mm(a, b):
    return _dot(a.astype(BF16), b.astype(BF16))


def _causal_conv_silu(ref, hh, r0, first, cw):
    cur = ref[0, 0, hh, pl.ds(r0, GDN_C), :]
    prev = ref[0, 0, hh, pl.ds(jnp.maximum(r0 - 8, 0), 8), :]
    prev = jnp.where(first, 0.0, prev)
    ext = jnp.concatenate([prev, cur], axis=0)
    acc = cur * cw[CONV_K - 1:CONV_K, :]
    for k in range(1, CONV_K):
        acc = acc + ext[8 - k:8 - k + GDN_C, :] * cw[CONV_K - 1 - k:CONV_K - k, :]
    return _silu(acc)


def _gdn_kernel(nq_ref, nk_ref, nv_ref, ng_ref, tail_ref, arow_ref, cwq_ref, cwk_ref, cwv_ref,
                alog_ref, dtb_ref, gn_ref, o_ref, gcr_ref):
    c = GDN_C
    n_chunks = nq_ref.shape[3] // c
    ri = lax.broadcasted_iota(jnp.int32, (c, c), 0)
    ci = lax.broadcasted_iota(jnp.int32, (c, c), 1)
    tril = ci <= ri
    strict = ci < ri
    tril_b = _ones_where(tril)
    triu_b = _ones_where(ri <= ci)
    gn = gn_ref[...]

    for hh in range(GDN_HEADS_PER_STEP):
        coef = -jnp.exp(alog_ref[hh])
        g_rows = coef[:, :1] * _softplus(arow_ref[hh, 0] + dtb_ref[hh][:, :1])
        g_rows = jnp.concatenate([g_rows, jnp.zeros((16 - n_chunks, c), F32)], axis=0)
        gcr_ref[hh] = _dot_exact_lhs(g_rows, triu_b)

    def chunk(n, states):
        return tuple(head_chunk(hh, n, states[hh]) for hh in range(GDN_HEADS_PER_STEP))

    def head_chunk(hh, n, state):
        h = pl.program_id(1) * GDN_HEADS_PER_STEP + hh
        coef = -jnp.exp(alog_ref[hh])
        dtb = dtb_ref[hh]
        lanes = slice(hh * HEAD_LANES, (hh + 1) * HEAD_LANES)
        sel_r = lax.broadcasted_iota(jnp.int32, (HEAD_LANES, 2 * HEAD_LANES), 0)
        sel_c = lax.broadcasted_iota(jnp.int32, (HEAD_LANES, 2 * HEAD_LANES), 1)
        sel = _ones_where(sel_r == jnp.where(sel_c < HEAD_LANES, h, DN_HEADS + h))
        r0 = pl.multiple_of(n * c, c)
        first = n == 0
        cq = _causal_conv_silu(nq_ref, hh, r0, first, cwq_ref[:, lanes])
        ck = _causal_conv_silu(nk_ref, hh, r0, first, cwk_ref[:, lanes])
        cv = _causal_conv_silu(nv_ref, hh, r0, first, cwv_ref[:, lanes])
        qn = (cq * lax.rsqrt(jnp.sum(cq * cq, axis=-1, keepdims=True) + EPS)) * (DN_HEAD_DIM ** -0.5)
        kn = ck * lax.rsqrt(jnp.sum(ck * ck, axis=-1, keepdims=True) + EPS)

        ba = _dot_exact_lhs(tail_ref[pl.ds(r0, c), :], sel)
        beta = jax.nn.sigmoid(ba[:, :HEAD_LANES])
        g_col = coef * _softplus(ba[:, HEAD_LANES:] + dtb)
        gc = _dot_exact_rhs(tril_b, g_col)
        gl = gc[c - 1:c, :]
        eg = jnp.exp(gc)
        g_row = gcr_ref[hh, pl.ds(n, 1), :]

        kb = kn * beta
        vb = cv * beta
        kkqk = _dot(jnp.concatenate([kb, qn], axis=0).astype(BF16), kn.astype(BF16), NT)
        diff = jnp.concatenate([gc, gc], axis=1) - g_row
        decay = jnp.exp(jnp.where(tril, diff, -jnp.inf))
        lmat = jnp.where(strict, kkqk[:c] * decay, 0.0)
        attn = kkqk[c:] * decay

        rb = ri // GDN_BASE
        cb = ci // GDN_BASE
        p = jnp.where(rb == cb, lmat, 0.0)
        tinv = jnp.where(ri == ci, 1.0, 0.0) - p
        for _ in range(int(math.log2(GDN_BASE)) - 1):
            p = _mm(p, p)
            tinv = tinv + _mm(tinv, p)
        blk = GDN_BASE
        while blk < c:
            off = jnp.where((ri // (2 * blk) == ci // (2 * blk)) & (ri // blk != ci // blk), lmat, 0.0)
            tinv = tinv - _mm(_mm(tinv, off), tinv)
            blk *= 2
        x = _mm(tinv, jnp.concatenate([vb, kb * eg], axis=1))
        u = x[:, :HEAD_LANES]
        w = x[:, HEAD_LANES:]

        sb = state.astype(BF16)
        v_new = u - _dot(w.astype(BF16), sb)
        vnb = v_new.astype(BF16)
        out = _dot((qn * eg).astype(BF16), sb) + _dot(attn.astype(BF16), vnb)
        k_tail = (kn * jnp.exp(gl - gc)).astype(BF16)
        state = state * jnp.exp(gl) + _dot(k_tail, vnb, ((0,), (0,)))

        ms = jnp.mean(out * out, axis=-1, keepdims=True)
        y = (out * lax.rsqrt(ms + EPS)) * gn
        gate = ng_ref[0, 0, hh, pl.ds(r0, c), :]
        o_ref[0, pl.ds(r0, c), lanes] = (y * _silu(gate)).astype(o_ref.dtype)
        return state

    state0 = tuple(jnp.zeros((DN_HEAD_DIM, DN_HEAD_DIM), F32) for _ in range(GDN_HEADS_PER_STEP))
    lax.fori_loop(0, n_chunks, chunk, state0)


def _gated_deltanet(proj_main, tail, tail_t, conv_w, a_log_b, dt_bias_b, g_gnorm, batch, seq):
    c = GDN_C
    hps = GDN_HEADS_PER_STEP
    n_chunks = seq // c
    assert n_chunks <= 16 and DN_HEADS % hps == 0
    arow = tail_t.reshape(16, batch, n_chunks, c)
    head_spec = lambda g: pl.BlockSpec((1, 1, hps, seq, HEAD_LANES), lambda b, h: (g, b, h, 0, 0))
    cw_spec = lambda g: pl.BlockSpec((CONV_K, hps * HEAD_LANES), lambda b, h: (0, g * (DN_HEADS // hps) + h))
    vec_spec = pl.BlockSpec((hps, 1, HEAD_LANES), lambda b, h: (h, 0, 0))
    return pl.pallas_call(
        _gdn_kernel,
        grid=(batch, DN_HEADS // hps),
        in_specs=[
            head_spec(3), head_spec(4), head_spec(5), head_spec(6),
            pl.BlockSpec((seq, HEAD_LANES), lambda b, h: (b, 0)),
            pl.BlockSpec((hps, 1, n_chunks, c), lambda b, h: (DN_HEADS // hps + h, b, 0, 0)),
            cw_spec(0), cw_spec(1), cw_spec(2),
            vec_spec, vec_spec,
            pl.BlockSpec((1, DN_HEAD_DIM), lambda b, h: (0, 0)),
        ],
        out_specs=pl.BlockSpec((1, seq, hps * HEAD_LANES), lambda b, h: (b, 0, h)),
        out_shape=jax.ShapeDtypeStruct((batch, seq, DN_HEADS * DN_HEAD_DIM), BF16),
        scratch_shapes=[pltpu.VMEM((hps, 16, c), F32)],
        compiler_params=_cparams(("arbitrary", "arbitrary")),
        name="gated_deltanet",
    )(proj_main, proj_main, proj_main, proj_main, tail, arow, conv_w, conv_w, conv_w,
      a_log_b, dt_bias_b, g_gnorm)


OUT_TM = 512
OUT_TK = 512
EPI_ROWS = 256


def _pack_bf16_halves(x):
    n = x.shape[1] // 2
    bits = pltpu.bitcast(x.astype(BF16).astype(F32), jnp.int32)
    lo = lax.shift_right_logical(bits[:, :n], 16)
    hi = jnp.bitwise_and(bits[:, n:], jnp.int32(-65536))
    return jnp.bitwise_or(hi, lo)


def _unpack_bf16_halves(w):
    lo = pltpu.bitcast(lax.shift_left(w, 16), F32).astype(BF16)
    hi = pltpu.bitcast(jnp.bitwise_and(w, jnp.int32(-65536)), F32).astype(BF16)
    return lo, hi


def _dot_f32(a, b_parts):
    a_hi, a_mid, a_lo = _split3(a)
    b_hi, b_mid, b_lo = b_parts
    acc = _dot(a_hi, b_hi)
    acc += _dot(a_hi, b_mid) + _dot(a_mid, b_hi)
    acc += _dot(a_hi, b_lo) + _dot(a_mid, b_mid) + _dot(a_lo, b_hi)
    return acc


def _outproj_kernel(od_ref, on_ref, w_ref, x_ref, g1_ref, gf_ref, sc_ref, sh_ref, wr_ref, br_ref,
                    x1_ref, hn2_ref, idx_ref, gate_ref, rank_ref, cnt_ref, acc_ref, mh_ref, carry_ref):
    i = pl.program_id(0)
    k = pl.program_id(1)
    nk = pl.num_programs(1)
    tm = OUT_TM

    @pl.when((i == 0) & (k == 0))
    def _():
        carry_ref[...] = jnp.zeros_like(carry_ref)

    lhs = jnp.where(k < nk // 2, od_ref[...], on_ref[...])
    part = _dot(lhs, w_ref[...])

    @pl.when(k == 0)
    def _():
        acc_ref[...] = part

    @pl.when(k > 0)
    def _():
        acc_ref[...] += part

    @pl.when(k == nk - 1)
    def _():
        g1 = g1_ref[0]
        gf = gf_ref[...]
        sc = sc_ref[0]
        sh = sh_ref[0]
        wr_parts = _split3(wr_ref[...])
        br = br_ref[...]
        lane = lax.broadcasted_iota(jnp.int32, (EPI_ROWS, HEAD_LANES), 1)
        for r in range(0, tm, EPI_ROWS):
            rows = slice(r, r + EPI_ROWS)
            x1 = x_ref[rows, :] + g1 * acc_ref[rows, :]
            x1_ref[rows, :] = x1
            hn2 = _norm_modulate(x1, gf, sc, sh)
            hn2_ref[rows, :] = _pack_bf16_halves(hn2)
            logits = _dot_f32(hn2, wr_parts) + br
            l = jnp.where(lane < N_EXPERTS, logits, -jnp.inf)
            vals, idxs = [], []
            for _ in range(TOP_K):
                m = jnp.max(l, axis=-1, keepdims=True)
                ix = jnp.min(jnp.where(l == m, lane, HEAD_LANES), axis=-1, keepdims=True)
                vals.append(m)
                idxs.append(ix)
                l = jnp.where(lane == ix, -jnp.inf, l)
            es = [jnp.exp(v - vals[0]) for v in vals]
            den = (es[0] + es[1]) + (es[2] + es[3])
            idx_o = jnp.zeros((EPI_ROWS, HEAD_LANES), jnp.int32)
            gate_o = jnp.zeros((EPI_ROWS, HEAD_LANES), F32)
            hot = jnp.zeros((EPI_ROWS, HEAD_LANES), F32)
            for kk in range(TOP_K):
                idx_o = jnp.where(lane == kk, idxs[kk], idx_o)
                gate_o = jnp.where(lane == kk, es[kk] / den, gate_o)
                hot = jnp.where(lane == idxs[kk], 1.0, hot)
            idx_ref[rows, :] = idx_o
            gate_ref[rows, :] = gate_o
            mh_ref[rows, :] = hot.astype(BF16)

        ri = lax.broadcasted_iota(jnp.int32, (tm, tm), 0)
        ci = lax.broadcasted_iota(jnp.int32, (tm, tm), 1)
        mh = mh_ref[...]
        carry = carry_ref[0:1, :]
        cum = _dot(_ones_where(ci < ri), mh) + carry
        lane_t = lax.broadcasted_iota(jnp.int32, (tm, HEAD_LANES), 1)
        idx_t = idx_ref[...]
        rank_o = jnp.zeros((tm, HEAD_LANES), F32)
        for kk in range(TOP_K):
            ik = idx_t[:, kk:kk + 1]
            rk = jnp.sum(jnp.where(lane_t == ik, cum, 0.0), axis=-1, keepdims=True)
            rank_o = jnp.where(lane_t == kk, rk, rank_o)
        rank_ref[...] = rank_o.astype(jnp.int32)
        total = carry + jnp.sum(mh.astype(F32), axis=0, keepdims=True)
        carry_ref[...] = jnp.broadcast_to(total, carry_ref.shape)
        cnt_ref[...] = jnp.broadcast_to(total, cnt_ref.shape).astype(jnp.int32)


def _outproj_router(o_diff, o_dn, w_out, x2, gate1, g_ffn, scale2, shift2, w_router_pad, b_router_pad, seq):
    n_tok, d = x2.shape
    tm, tk = OUT_TM, OUT_TK
    tiles_per_batch = seq // tm
    nk = d // tk
    half = nk // 2
    vec3 = pl.BlockSpec((1, 1, d), lambda i, k: (i // tiles_per_batch, 0, 0))
    row_out = lambda dt: jax.ShapeDtypeStruct((n_tok, HEAD_LANES), dt)
    return pl.pallas_call(
        _outproj_kernel,
        grid=(n_tok // tm, nk),
        in_specs=[
            pl.BlockSpec((tm, tk), lambda i, k: (i, jnp.minimum(k, half - 1))),
            pl.BlockSpec((tm, tk), lambda i, k: (i, jnp.maximum(k - half, 0))),
            pl.BlockSpec((tk, d), lambda i, k: (k, 0)),
            pl.BlockSpec((tm, d), lambda i, k: (i, 0)),
            vec3,
            pl.BlockSpec((1, d), lambda i, k: (0, 0)),
            vec3, vec3,
            pl.BlockSpec((d, HEAD_LANES), lambda i, k: (0, 0)),
            pl.BlockSpec((1, HEAD_LANES), lambda i, k: (0, 0)),
        ],
        out_specs=[
            pl.BlockSpec((tm, d), lambda i, k: (i, 0)),
            pl.BlockSpec((tm, d // 2), lambda i, k: (i, 0)),
            pl.BlockSpec((tm, HEAD_LANES), lambda i, k: (i, 0)),
            pl.BlockSpec((tm, HEAD_LANES), lambda i, k: (i, 0)),
            pl.BlockSpec((tm, HEAD_LANES), lambda i, k: (i, 0)),
            pl.BlockSpec((8, HEAD_LANES), lambda i, k: (0, 0)),
        ],
        out_shape=[
            jax.ShapeDtypeStruct((n_tok, d), F32),
            jax.ShapeDtypeStruct((n_tok, d // 2), jnp.int32),
            row_out(jnp.int32), row_out(F32), row_out(jnp.int32),
            jax.ShapeDtypeStruct((8, HEAD_LANES), jnp.int32),
        ],
        scratch_shapes=[pltpu.VMEM((tm, d), F32), pltpu.VMEM((tm, HEAD_LANES), BF16),
                        pltpu.VMEM((8, HEAD_LANES), F32)],
        compiler_params=_cparams(("arbitrary", "arbitrary")),
        name="outproj_router",
    )(o_diff, o_dn, w_out, x2, gate1, g_ffn, scale2, shift2, w_router_pad, b_router_pad)


MOE_TM = 2048
MOE_SUB = 256
MOE_CHUNK = 512
MOE_TF = 256
MOE_TN = 512
DSP_TM = 256


def _row_copy(src_hbm, row, dst_vmem, slot, sem):
    return pltpu.make_async_copy(src_hbm.at[pl.ds(row, 1)], dst_vmem.at[pl.ds(slot, 1)], sem)


def _dispatch_kernel(sbr_ref, sbm_ref, dest_ref, hn2_ref, xs_hbm, zero_ref, sem):
    i = pl.program_id(0)
    tm = DSP_TM

    @pl.when(i == 0)
    def _():
        zero_ref[...] = jnp.zeros_like(zero_ref)
        n_sb = sbr_ref.shape[0]

        def pad_copies(s, fn):
            nv = sbr_ref[s]
            nv8 = (nv + 7) // 8 * 8
            for t in range(7):
                @pl.when(nv + t < nv8)
                def _():
                    fn(pltpu.make_async_copy(zero_ref.at[pl.ds(0, 1)],
                                             xs_hbm.at[pl.ds(s * MOE_TM + nv + t, 1)], sem))

            pad8 = (sbm_ref[s] - nv8) // 8
            bit = MOE_SUB // 16
            while bit >= 1:
                rows = 8 * bit
                start = pl.multiple_of(s * MOE_TM + nv8 + 8 * (pad8 & ~(2 * bit - 1)), 8)

                @pl.when((pad8 & bit) != 0)
                def _():
                    fn(pltpu.make_async_copy(zero_ref.at[pl.ds(0, rows)],
                                             xs_hbm.at[pl.ds(start, rows)], sem))

                bit //= 2
            return 0

        lax.fori_loop(0, n_sb, lambda s, c: c + pad_copies(s, lambda cp: cp.start()), 0)
        lax.fori_loop(0, n_sb, lambda s, c: c + pad_copies(s, lambda cp: cp.wait()), 0)

    def issue(r, carry):
        for kk in range(TOP_K):
            pltpu.make_async_copy(hn2_ref.at[pl.ds(r, 1)],
                                  xs_hbm.at[pl.ds(dest_ref[0, 0, kk * tm + r], 1)], sem).start()
        return carry

    lax.fori_loop(0, tm, issue, 0)

    def wait(r, carry):
        for kk in range(TOP_K):
            pltpu.make_async_copy(hn2_ref.at[pl.ds(0, 1)], xs_hbm.at[pl.ds(0, 1)], sem).wait()
        return carry

    lax.fori_loop(0, tm, wait, 0)


def _dispatch(sb_rows, sb_m, dest_tiles, hn2, n_rows):
    n_tok, d = hn2.shape
    tm = DSP_TM
    grid_spec = pltpu.PrefetchScalarGridSpec(
        num_scalar_prefetch=2,
        grid=(n_tok // tm,),
        in_specs=[
            pl.BlockSpec((1, 1, tm * TOP_K), lambda i, sbr, sbm: (i, 0, 0), memory_space=pltpu.SMEM),
            pl.BlockSpec((tm, d), lambda i, sbr, sbm: (i, 0)),
        ],
        out_specs=pl.BlockSpec(memory_space=pl.ANY),
        scratch_shapes=[pltpu.VMEM((MOE_SUB // 2, d), hn2.dtype), pltpu.SemaphoreType.DMA(())],
    )
    return pl.pallas_call(
        _dispatch_kernel,
        grid_spec=grid_spec,
        out_shape=jax.ShapeDtypeStruct((n_rows, d), hn2.dtype),
        compiler_params=_cparams(("arbitrary",)),
        name="moe_dispatch",
    )(sb_rows, sb_m, dest_tiles, hn2)


def _for_row_chunks(m, body):
    n_full = m // MOE_CHUNK

    def step(c, carry):
        body(pl.multiple_of(c * MOE_CHUNK, MOE_CHUNK), MOE_CHUNK)
        return carry

    lax.fori_loop(0, n_full, step, 0)
    assert MOE_CHUNK == 2 * MOE_SUB

    @pl.when(m - n_full * MOE_CHUNK > 0)
    def _():
        body(pl.multiple_of(n_full * MOE_CHUNK, MOE_CHUNK), MOE_SUB)


def _moe_up_kernel(sbe_ref, sbm_ref, nused_ref, x_ref, wg_ref, wu_ref, bg_ref, bu_ref, h_ref,
                   xb_ref, wgb_ref, wub_ref):
    s = pl.program_id(0)
    f = pl.program_id(1)
    m = sbm_ref[s]
    half = x_ref.shape[1]

    @pl.when(f == 0)
    def _():
        def unpack(r0, rows):
            lo, hi = _unpack_bf16_halves(x_ref[pl.ds(r0, rows), :])
            xb_ref[pl.ds(r0, rows), 0:half] = lo
            xb_ref[pl.ds(r0, rows), half:2 * half] = hi

        _for_row_chunks(m, unpack)

    @pl.when(m > 0)
    def _():
        wgb_ref[...] = wg_ref[0].astype(BF16)
        wub_ref[...] = wu_ref[0].astype(BF16)

    def compute(r0, rows):
        xq = xb_ref[pl.ds(r0, rows), :]
        gate = jnp.minimum(_dot(xq, wgb_ref[...]) + bg_ref[0], SWIGLU_LIMIT)
        up = jnp.clip(_dot(xq, wub_ref[...]) + bu_ref[0], -SWIGLU_LIMIT, SWIGLU_LIMIT)
        glu = gate * jax.nn.sigmoid(gate * SWIGLU_ALPHA)
        h_ref[pl.ds(r0, rows), :] = ((up + 1.0) * glu).astype(BF16)

    _for_row_chunks(m, compute)


def _moe_up(sb_expert, sb_m, n_used, xs, w_gate_up, b_gate_up):
    n_sb = sb_expert.shape[0]
    d = xs.shape[1]
    n_exp, dm, f2 = w_gate_up.shape
    assert dm == 2 * d
    f_dim = f2 // 2
    nf = f_dim // MOE_TF

    def f_eff(s, f, nu):
        return jnp.where(s < nu[0], f, nf - 1)

    def s_eff(s, nu):
        return jnp.minimum(s, nu[0] - 1)

    grid_spec = pltpu.PrefetchScalarGridSpec(
        num_scalar_prefetch=3,
        grid=(n_sb, nf),
        in_specs=[
            pl.BlockSpec((MOE_TM, d), lambda s, f, sbe, sbm, nu: (s_eff(s, nu), 0)),
            pl.BlockSpec((1, dm, MOE_TF), lambda s, f, sbe, sbm, nu: (sbe[s], 0, f_eff(s, f, nu))),
            pl.BlockSpec((1, dm, MOE_TF), lambda s, f, sbe, sbm, nu: (sbe[s], 0, nf + f_eff(s, f, nu))),
            pl.BlockSpec((1, 1, MOE_TF), lambda s, f, sbe, sbm, nu: (sbe[s], 0, f_eff(s, f, nu))),
            pl.BlockSpec((1, 1, MOE_TF), lambda s, f, sbe, sbm, nu: (sbe[s], 0, nf + f_eff(s, f, nu))),
        ],
        out_specs=pl.BlockSpec((MOE_TM, MOE_TF), lambda s, f, sbe, sbm, nu: (s_eff(s, nu), f_eff(s, f, nu))),
        scratch_shapes=[pltpu.VMEM((MOE_TM, 2 * d), BF16), pltpu.VMEM((2 * d, MOE_TF), BF16),
                        pltpu.VMEM((2 * d, MOE_TF), BF16)],
    )
    b3 = b_gate_up.reshape(n_exp, 1, f2)
    return pl.pallas_call(
        _moe_up_kernel,
        grid_spec=grid_spec,
        out_shape=jax.ShapeDtypeStruct((n_sb * MOE_TM, f_dim), BF16),
        compiler_params=_cparams(("arbitrary", "arbitrary")),
        name="moe_up",
    )(sb_expert, sb_m, n_used, xs, w_gate_up, w_gate_up, b3, b3)


def _moe_down_kernel(sbe_ref, sbm_ref, nused_ref, h_ref, wd_ref, bd_ref, o_ref, wdb_ref):
    s = pl.program_id(0)
    m = sbm_ref[s]

    @pl.when(m > 0)
    def _():
        wdb_ref[...] = wd_ref[0].astype(BF16)

    def compute(r0, rows):
        o_ref[pl.ds(r0, rows), :] = _dot(h_ref[pl.ds(r0, rows), :], wdb_ref[...]) + bd_ref[0]

    _for_row_chunks(m, compute)


def _moe_down(sb_expert, sb_m, n_used, h, w_down, b_down):
    n_sb = sb_expert.shape[0]
    n_exp, f_dim, d = w_down.shape
    nn = d // MOE_TN

    def n_eff(s, n, nu):
        return jnp.where(s < nu[0], n, nn - 1)

    def s_eff(s, nu):
        return jnp.minimum(s, nu[0] - 1)

    grid_spec = pltpu.PrefetchScalarGridSpec(
        num_scalar_prefetch=3,
        grid=(n_sb, nn),
        in_specs=[
            pl.BlockSpec((MOE_TM, f_dim), lambda s, n, sbe, sbm, nu: (s_eff(s, nu), 0)),
            pl.BlockSpec((1, f_dim, MOE_TN), lambda s, n, sbe, sbm, nu: (sbe[s], 0, n_eff(s, n, nu))),
            pl.BlockSpec((1, 1, MOE_TN), lambda s, n, sbe, sbm, nu: (sbe[s], 0, n_eff(s, n, nu))),
        ],
        out_specs=pl.BlockSpec((MOE_TM, MOE_TN), lambda s, n, sbe, sbm, nu: (s_eff(s, nu), n_eff(s, n, nu))),
        scratch_shapes=[pltpu.VMEM((f_dim, MOE_TN), BF16)],
    )
    return pl.pallas_call(
        _moe_down_kernel,
        grid_spec=grid_spec,
        out_shape=jax.ShapeDtypeStruct((n_sb * MOE_TM, d), F32),
        compiler_params=_cparams(("arbitrary", "arbitrary")),
        name="moe_down",
    )(sb_expert, sb_m, n_used, h, w_down, b_down.reshape(n_exp, 1, d))


CMB_TM = 256


def _combine_kernel(dest_ref, ys_hbm, x1_ref, gate_ref, g2_ref, gfin_ref, o_ref, buf_ref, sem):
    tm = CMB_TM

    def issue(r, carry):
        for kk in range(TOP_K):
            _row_copy(ys_hbm, dest_ref[0, 0, kk * tm + r], buf_ref.at[kk], r, sem).start()
        return carry

    lax.fori_loop(0, tm, issue, 0)

    def wait(r, carry):
        for kk in range(TOP_K):
            _row_copy(ys_hbm, 0, buf_ref.at[kk], 0, sem).wait()
        return carry

    lax.fori_loop(0, tm, wait, 0)

    gates = gate_ref[...]
    y = gates[:, 0:1] * buf_ref[0]
    for kk in range(1, TOP_K):
        y = y + gates[:, kk:kk + 1] * buf_ref[kk]
    x2 = x1_ref[...] + g2_ref[0] * y
    ms = jnp.mean(x2 * x2, axis=-1, keepdims=True)
    o_ref[...] = (x2 * lax.rsqrt(ms + EPS)) * gfin_ref[...]


def _combine(dest_tiles, ys, x1, gates, gate2, g_final, seq):
    n_tok, d = x1.shape
    tm = CMB_TM
    tiles_per_batch = seq // tm
    return pl.pallas_call(
        _combine_kernel,
        grid=(n_tok // tm,),
        in_specs=[
            pl.BlockSpec((1, 1, tm * TOP_K), lambda i: (i, 0, 0), memory_space=pltpu.SMEM),
            pl.BlockSpec(memory_space=pl.ANY),
            pl.BlockSpec((tm, d), lambda i: (i, 0)),
            pl.BlockSpec((tm, HEAD_LANES), lambda i: (i, 0)),
            pl.BlockSpec((1, 1, d), lambda i: (i // tiles_per_batch, 0, 0)),
            pl.BlockSpec((1, d), lambda i: (0, 0)),
        ],
        out_specs=pl.BlockSpec((tm, d), lambda i: (i, 0)),
        out_shape=jax.ShapeDtypeStruct((n_tok, d), F32),
        scratch_shapes=[pltpu.VMEM((TOP_K, tm, d), F32), pltpu.SemaphoreType.DMA(())],
        compiler_params=_cparams(("arbitrary",)),
        name="moe_combine_norm",
    )(dest_tiles, ys, x1, gates, gate2, g_final)


def _route_tables(top_idx, rank, counts, n_tok):
    n_sb = n_tok * TOP_K // MOE_TM + N_EXPERTS
    max_sb_per_e = n_tok // MOE_TM
    cnt = counts[0, :N_EXPERTS]
    sb_per_e = (cnt + MOE_TM - 1) // MOE_TM
    rows_per_sb = (cnt + jnp.maximum(sb_per_e, 1) - 1) // jnp.maximum(sb_per_e, 1)
    sb_end = jnp.cumsum(sb_per_e)
    sb_start = sb_end - sb_per_e
    n_used = sb_end[-1]
    s_ids = jnp.arange(n_sb, dtype=jnp.int32)
    e_ids = jnp.arange(N_EXPERTS, dtype=jnp.int32)
    s_clamped = jnp.minimum(s_ids, n_used - 1)
    e_of_s = jnp.sum((sb_end[None, :] <= s_clamped[:, None]).astype(jnp.int32), axis=1)
    e_of_s = jnp.minimum(e_of_s, N_EXPERTS - 1).astype(jnp.int32)
    onehot_s = (e_of_s[:, None] == e_ids[None, :]).astype(jnp.int32)
    j_of_s = s_ids - jnp.sum(onehot_s * sb_start[None, :], axis=1)
    per_s = jnp.sum(onehot_s * rows_per_sb[None, :], axis=1)
    cnt_s = jnp.sum(onehot_s * cnt[None, :], axis=1)
    sb_rows = jnp.where(s_ids < n_used, jnp.clip(cnt_s - j_of_s * per_s, 0, per_s), 0).astype(jnp.int32)
    sb_m = ((sb_rows + MOE_SUB - 1) // MOE_SUB * MOE_SUB).astype(jnp.int32)
    idx_t = top_idx[:, :TOP_K].T
    rank_t = rank[:, :TOP_K].T
    onehot_a = idx_t[None, :, :] == e_ids[:, None, None]
    start_a = jnp.sum(jnp.where(onehot_a, sb_start[:, None, None], 0), axis=0)
    per_a = jnp.sum(jnp.where(onehot_a, rows_per_sb[:, None, None], 0), axis=0)
    sb_in_e = sum((rank_t >= j * per_a).astype(jnp.int32) for j in range(1, max_sb_per_e))
    dest_t = ((start_a + sb_in_e) * MOE_TM + (rank_t - sb_in_e * per_a)).astype(jnp.int32)
    return e_of_s, sb_rows, sb_m, n_used.reshape(1).astype(jnp.int32), dest_t, n_sb


def _moe_block(hn2_packed, top_idx, rank, counts, x1, gates, gate2, w_gate_up, b_gate_up, w_down, b_down,
               g_final, seq):
    n_tok = x1.shape[0]
    e_of_s, sb_rows, sb_m, n_used1, dest_t, n_sb = _route_tables(top_idx, rank, counts, n_tok)

    def dest_tiles(tm):
        return dest_t.reshape(TOP_K, n_tok // tm, tm).transpose(1, 0, 2).reshape(n_tok // tm, 1, TOP_K * tm)

    xs = _dispatch(sb_rows, sb_m, dest_tiles(DSP_TM), hn2_packed, n_sb * MOE_TM)
    hmid = _moe_up(e_of_s, sb_m, n_used1, xs, w_gate_up, b_gate_up)
    ys = _moe_down(e_of_s, sb_m, n_used1, hmid, w_down, b_down)
    return _combine(dest_tiles(CMB_TM), ys, x1, gates, gate2, g_final[None, :], seq)


def kernel(x, c, w_ada, b_ada, g_norm_mix, g_norm_ffn, g_norm_final, w_in, conv_w, rel_bias,
           lambda_q1, lambda_k1, lambda_q2, lambda_k2, g_subln, a_log, dt_bias, g_gnorm, w_out,
           w_router, b_router, w_gate_up, b_gate_up, w_down, b_down):
    batch, seq, d = x.shape
    n_tok = batch * seq
    depth = w_ada.shape[0]
    assert depth == 1 and d == D_MODEL
    x2 = x.reshape(n_tok, d)
    bias_tiles = _bias_tiles(rel_bias)
    for l in range(depth):
        c_pad = jnp.zeros((16, d), F32).at[:batch].set(c)
        mod = _adaln(c_pad, w_ada[l], b_ada[l][None, :])[:batch]
        shift1, scale1, gate1, shift2, scale2, gate2 = [m.reshape(batch, 1, d) for m in jnp.split(mod, 6, axis=-1)]

        w_tail = jnp.pad(w_in[l, :, MAIN_COLS:], ((0, 0), (0, HEAD_LANES - 2 * DN_HEADS)))
        w_tail_t = w_in[l, :, MAIN_COLS:].T
        proj_main, tail, tail_t = _inproj(x2, g_norm_mix[l][None, :], scale1, shift1,
                                          _to_bf16(w_in, l, MAIN_COLS), w_tail, w_tail_t, batch, seq)
        o_diff = _diff_attention(proj_main, bias_tiles, lambda_q1[l][None, :], lambda_k1[l][None, :],
                                 lambda_q2[l][None, :], lambda_k2[l][None, :], g_subln[l][None, :], batch, seq)
        a_log_b = jnp.broadcast_to(a_log[l][:, None, None], (DN_HEADS, 1, HEAD_LANES))
        dt_bias_b = jnp.broadcast_to(dt_bias[l][:, None, None], (DN_HEADS, 1, HEAD_LANES))
        o_dn = _gated_deltanet(proj_main, tail, tail_t, conv_w[l], a_log_b, dt_bias_b, g_gnorm[l][None, :],
                               batch, seq)

        w_router_pad = jnp.pad(w_router[l], ((0, 0), (0, HEAD_LANES - N_EXPERTS)))
        b_router_pad = jnp.pad(b_router[l], (0, HEAD_LANES - N_EXPERTS))[None, :]
        x1, hn2, top_idx, gates, rank, counts = _outproj_router(
            o_diff.reshape(n_tok, -1), o_dn.reshape(n_tok, -1), _to_bf16(w_out, l, d), x2, gate1, g_norm_ffn[l][None, :],
            scale2, shift2, w_router_pad, b_router_pad, seq)

        x2 = _moe_block(hn2, top_idx, rank, counts, x1, gates, gate2, w_gate_up[l], b_gate_up[l], w_down[l],
                        b_down[l], g_norm_final, seq)
    return x2.reshape(batch, seq, d)
```

```python
import functools
import math

import jax
import jax.numpy as jnp
import numpy as np
from jax import lax
from jax.experimental import pallas as pl
from jax.experimental.pallas import tpu as pltpu

F32 = jnp.float32
BF16 = jnp.bfloat16

D_MODEL = 2048
DIFF_HEADS = 8
DIFF_HEAD_DIM = 64
DIFF_V_DIM = 128
DN_HEADS = 8
DN_HEAD_DIM = 128
HEAD_LANES = 128
CONV_K = 4
N_BUCKETS = 32
MAX_DISTANCE = 128
N_EXPERTS = 32
TOP_K = 4
D_EXPERT = 2048
SWIGLU_LIMIT = 7.0
SWIGLU_ALPHA = 1.702
EPS = 1e-6
LAMBDA_INIT = 0.8 - 0.6 * math.exp(-0.3 * 0)
N_MAIN_GROUPS = 7
MAIN_COLS = N_MAIN_GROUPS * 1024

V7X_VMEM_BYTES = 64 * 1024 * 1024
VMEM_LIMIT = 56 * 1024 * 1024

NEG_BIG = -1e30


def _cparams(sem):
    return pltpu.CompilerParams(dimension_semantics=sem, vmem_limit_bytes=VMEM_LIMIT)


def _split3(a):
    hi = a.astype(BF16)
    r1 = a - hi.astype(F32)
    mid = r1.astype(BF16)
    lo = (r1 - mid.astype(F32)).astype(BF16)
    return hi, mid, lo


def _dot(a, b, dims=None):
    if dims is None:
        return jnp.dot(a, b, preferred_element_type=F32)
    return lax.dot_general(a, b, (dims, ((), ())), preferred_element_type=F32)


def _dot_exact_lhs(a, b_bf16, dims=None):
    hi, mid, lo = _split3(a)
    return (_dot(hi, b_bf16, dims) + _dot(mid, b_bf16, dims)) + _dot(lo, b_bf16, dims)


def _dot_exact_rhs(a_bf16, b, dims=None):
    hi, mid, lo = _split3(b)
    return (_dot(a_bf16, hi, dims) + _dot(a_bf16, mid, dims)) + _dot(a_bf16, lo, dims)


NT = ((1,), (1,))


def _ones_where(mask):
    return jnp.where(mask, 1.0, 0.0).astype(BF16)


ADA_TN = 1024


def _adaln_kernel(c_ref, w_ref, b_ref, o_ref):
    c = c_ref[...]
    sc = c * jax.nn.sigmoid(c)
    a_hi, a_mid, a_lo = _split3(sc)
    w_hi, w_mid, w_lo = _split3(w_ref[...])
    acc = _dot(a_hi, w_hi)
    acc += _dot(a_hi, w_mid) + _dot(a_mid, w_hi)
    acc += _dot(a_hi, w_lo) + _dot(a_mid, w_mid) + _dot(a_lo, w_hi)
    o_ref[...] = acc + b_ref[...]


def _adaln(c_pad, w, b):
    n = w.shape[1]
    return pl.pallas_call(
        _adaln_kernel,
        grid=(n // ADA_TN,),
        in_specs=[
            pl.BlockSpec(c_pad.shape, lambda j: (0, 0)),
            pl.BlockSpec((w.shape[0], ADA_TN), lambda j: (0, j)),
            pl.BlockSpec((1, ADA_TN), lambda j: (0, j)),
        ],
        out_specs=pl.BlockSpec((c_pad.shape[0], ADA_TN), lambda j: (0, j)),
        out_shape=jax.ShapeDtypeStruct((c_pad.shape[0], n), F32),
        compiler_params=_cparams(("arbitrary",)),
        name="adaln_mod",
    )(c_pad, w, b)


INPROJ_TM = 1024
INPROJ_TN = 1024
NORM_ROWS = 256


def _norm_modulate(x, g, scale, shift):
    ms = jnp.mean(x * x, axis=-1, keepdims=True)
    y = (x * lax.rsqrt(ms + EPS)) * g
    return y * (1.0 + scale) + shift


def _inproj_kernel(x_ref, g_ref, sc_ref, sh_ref, w_ref, wt_ref, wtt_ref, o_ref, t_ref, tt_ref, hn_ref):
    j = pl.program_id(1)

    @pl.when(j == 0)
    def _():
        g = g_ref[...]
        sc = sc_ref[0]
        sh = sh_ref[0]
        for r in range(0, INPROJ_TM, NORM_ROWS):
            hn = _norm_modulate(x_ref[r:r + NORM_ROWS, :], g, sc, sh)
            hn_ref[r:r + NORM_ROWS, :] = hn.astype(BF16)
        hnb = hn_ref[...]
        t_ref[...] = _dot(hnb, wt_ref[...].astype(BF16))
        tt_ref[...] = _dot(wtt_ref[...].astype(BF16), hnb, NT)

    acc = _dot(hn_ref[...], w_ref[...])
    for hh in range(INPROJ_TN // HEAD_LANES):
        o_ref[0, 0, hh] = acc[:, hh * HEAD_LANES:(hh + 1) * HEAD_LANES]


def _inproj(x2, g_mix, scale1, shift1, w_in, w_tail, w_tail_t, batch, seq):
    n_tok, d = x2.shape
    tiles_per_batch = seq // INPROJ_TM
    heads_per_tile = INPROJ_TN // HEAD_LANES
    tiles_per_group = 1024 // INPROJ_TN
    grid = (n_tok // INPROJ_TM, MAIN_COLS // INPROJ_TN)
    return pl.pallas_call(
        _inproj_kernel,
        grid=grid,
        in_specs=[
            pl.BlockSpec((INPROJ_TM, d), lambda i, j: (i, 0)),
            pl.BlockSpec((1, d), lambda i, j: (0, 0)),
            pl.BlockSpec((1, 1, d), lambda i, j: (i // tiles_per_batch, 0, 0)),
            pl.BlockSpec((1, 1, d), lambda i, j: (i // tiles_per_batch, 0, 0)),
            pl.BlockSpec((d, INPROJ_TN), lambda i, j: (0, j)),
            pl.BlockSpec((d, HEAD_LANES), lambda i, j: (0, 0)),
            pl.BlockSpec((16, d), lambda i, j: (0, 0)),
        ],
        out_specs=[
            pl.BlockSpec((1, 1, heads_per_tile, INPROJ_TM, HEAD_LANES),
                         lambda i, j: (j // tiles_per_group, i // tiles_per_batch, j % tiles_per_group,
                                       i % tiles_per_batch, 0)),
            pl.BlockSpec((INPROJ_TM, HEAD_LANES), lambda i, j: (i, 0)),
            pl.BlockSpec((16, INPROJ_TM), lambda i, j: (0, i)),
        ],
        out_shape=[
            jax.ShapeDtypeStruct((N_MAIN_GROUPS, batch, 8, seq, HEAD_LANES), F32),
            jax.ShapeDtypeStruct((n_tok, HEAD_LANES), F32),
            jax.ShapeDtypeStruct((16, n_tok), F32),
        ],
        scratch_shapes=[pltpu.VMEM((INPROJ_TM, d), BF16)],
        compiler_params=_cparams(("arbitrary", "arbitrary")),
        name="norm_inproj",
    )(x2, g_mix, scale1, shift1, w_in, w_tail, w_tail_t)


ATT_T = 256
ATT_ROWS = 128


def _t5_bucket(n):
    n = jnp.maximum(n, 0)
    max_exact = N_BUCKETS // 2
    nf = jnp.maximum(n, max_exact).astype(F32)
    large = max_exact + (jnp.log(nf / max_exact) / math.log(MAX_DISTANCE / max_exact)
                         * (N_BUCKETS - max_exact)).astype(jnp.int32)
    large = jnp.minimum(large, N_BUCKETS - 1)
    return jnp.where(n < max_exact, n, large)


def _bias_tiles(rel_bias):
    assert ATT_T >= MAX_DISTANCE
    t = ATT_T
    table = rel_bias.astype(F32)
    n_heads = table.shape[1]
    tiles = []
    for d in range(2):
        dist = (d * t + t) - jnp.arange(2 * t, dtype=jnp.int32)
        w = jnp.where(dist[:, None] >= 0, table[_t5_bucket(dist)], F32(NEG_BIG)).T
        w_ext = jnp.concatenate([w, jnp.zeros((n_heads, 1), F32)], axis=1)
        skew = jnp.tile(w_ext, (1, t))[:, :t * 2 * t].reshape(n_heads, t, 2 * t)
        tiles.append(skew[:, :, t:])
    far = jnp.broadcast_to(table[N_BUCKETS - 1][:, None, None], (n_heads, t, t))
    return jnp.stack(tiles + [far], axis=1)


def _attn_kernel(q_ref, k_ref, v_ref, bias_ref, lq1_ref, lk1_ref, lq2_ref, lk2_ref, gs_ref,
                 o_ref, kb_ref, vb_ref, s0_ref, s1_ref, p_ref, m_ref, l_ref, acc_ref):
    i = pl.program_id(2)
    t = ATT_T
    rows2 = 2 * t

    @pl.when(i == 0)
    def _():
        kb_ref[...] = k_ref[0, 0, 0].astype(BF16)
        vb_ref[...] = v_ref[0, 0, 0].astype(BF16)

    q = q_ref[0, 0, 0] * (DIFF_HEAD_DIM ** -0.5)
    lane = lax.broadcasted_iota(jnp.int32, q.shape, 1)
    q1 = jnp.where(lane < DIFF_HEAD_DIM, q, 0.0)
    q2 = jnp.where(lane >= DIFF_HEAD_DIM, q, 0.0)
    q12 = jnp.concatenate([q1, q2], axis=0).astype(BF16)

    m_ref[...] = jnp.full(m_ref.shape, -jnp.inf, F32)
    l_ref[...] = jnp.zeros(l_ref.shape, F32)
    acc_ref[...] = jnp.zeros(acc_ref.shape, F32)

    def scores(j):
        return _dot(q12, kb_ref[pl.ds(pl.multiple_of(j * t, t), t), :], NT)

    def block(j, s_cur_ref, s_next_ref):
        s_next_ref[...] = scores(jnp.minimum(j + 1, i))
        d = jnp.minimum(i - j, 2)
        for r in range(0, rows2, ATT_ROWS):
            rows = slice(r, r + ATT_ROWS)
            brows = slice(r % t, r % t + ATT_ROWS)
            halves = [s_cur_ref[rows, c:c + HEAD_LANES] + bias_ref[0, d, brows, c:c + HEAD_LANES]
                      for c in range(0, t, HEAD_LANES)]
            m_old = m_ref[rows, :]
            m_new = jnp.maximum(m_old, jnp.max(functools.reduce(jnp.maximum, halves), axis=-1, keepdims=True))
            alpha = jnp.exp(m_old - m_new)
            ps = [jnp.exp(sh - m_new) for sh in halves]
            l_ref[rows, :] = alpha * l_ref[rows, :] + jnp.sum(functools.reduce(jnp.add, ps), axis=-1, keepdims=True)
            m_ref[rows, :] = m_new
            for c, ph in zip(range(0, t, HEAD_LANES), ps):
                p_ref[rows, c:c + HEAD_LANES] = ph.astype(BF16)
            acc_ref[rows, :] = alpha * acc_ref[rows, :]
        acc_ref[...] += _dot(p_ref[...], vb_ref[pl.ds(pl.multiple_of(j * t, t), t), :])

    s0_ref[...] = scores(0)

    def pair(pidx, carry):
        j0 = 2 * pidx
        block(j0, s0_ref, s1_ref)

        @pl.when(j0 + 1 <= i)
        def _():
            block(j0 + 1, s1_ref, s0_ref)

        return carry

    lax.fori_loop(0, (i + 2) // 2, pair, 0)

    lam = (jnp.exp(jnp.sum(lq1_ref[...] * lk1_ref[...], axis=-1, keepdims=True))
           - jnp.exp(jnp.sum(lq2_ref[...] * lk2_ref[...], axis=-1, keepdims=True)) + LAMBDA_INIT)
    o = acc_ref[0:t, :] / l_ref[0:t, :] - lam * (acc_ref[t:rows2, :] / l_ref[t:rows2, :])
    ms = jnp.mean(o * o, axis=-1, keepdims=True)
    y = (o * lax.rsqrt(ms + EPS)) * gs_ref[...]
    o_ref[0] = (y * (1.0 - LAMBDA_INIT)).astype(o_ref.dtype)


def _diff_attention(proj_main, bias_tiles, lq1, lk1, lq2, lk2, g_subln, batch, seq):
    t = ATT_T
    nq = seq // t
    lam_spec = pl.BlockSpec((1, DIFF_HEAD_DIM), lambda b, h, i: (0, 0))
    return pl.pallas_call(
        _attn_kernel,
        grid=(batch, DIFF_HEADS, nq),
        in_specs=[
            pl.BlockSpec((1, 1, 1, t, HEAD_LANES), lambda b, h, i: (0, b, h, i, 0)),
            pl.BlockSpec((1, 1, 1, seq, HEAD_LANES), lambda b, h, i: (1, b, h, 0, 0)),
            pl.BlockSpec((1, 1, 1, seq, HEAD_LANES), lambda b, h, i: (2, b, h, 0, 0)),
            pl.BlockSpec((1, 3, t, t), lambda b, h, i: (h, 0, 0, 0)),
            lam_spec, lam_spec, lam_spec, lam_spec,
            pl.BlockSpec((1, DIFF_V_DIM), lambda b, h, i: (0, 0)),
        ],
        out_specs=pl.BlockSpec((1, t, HEAD_LANES), lambda b, h, i: (b, i, h)),
        scratch_shapes=[
            pltpu.VMEM((seq, HEAD_LANES), BF16), pltpu.VMEM((seq, HEAD_LANES), BF16),
            pltpu.VMEM((2 * t, t), F32), pltpu.VMEM((2 * t, t), F32), pltpu.VMEM((2 * t, t), BF16),
            pltpu.VMEM((2 * t, HEAD_LANES), F32), pltpu.VMEM((2 * t, HEAD_LANES), F32),
            pltpu.VMEM((2 * t, DIFF_V_DIM), F32),
        ],
        out_shape=jax.ShapeDtypeStruct((batch, seq, DIFF_HEADS * DIFF_V_DIM), BF16),
        compiler_params=_cparams(("arbitrary", "arbitrary", "arbitrary")),
        name="diff_attention",
    )(proj_main, proj_main, proj_main, bias_tiles, lq1, lk1, lq2, lk2, g_subln)


GDN_C = 256
GDN_BASE = 16
GDN_HEADS_PER_STEP = 4


def _softplus(x):
    return jnp.maximum(x, 0.0) + jnp.log1p(jnp.exp(-jnp.abs(x)))


def _silu(x):
    return x * jax.nn.sigmoid(x)


def _mm(a, b):
    return _dot(a.astype(BF16), b.astype(BF16))


def _causal_conv_silu(ref, hh, r0, first, cw):
    cur = ref[0, 0, hh, pl.ds(r0, GDN_C), :]
    prev = ref[0, 0, hh, pl.ds(jnp.maximum(r0 - 8, 0), 8), :]
    prev = jnp.where(first, 0.0, prev)
    ext = jnp.concatenate([prev, cur], axis=0)
    acc = cur * cw[CONV_K - 1:CONV_K, :]
    for k in range(1, CONV_K):
        acc = acc + ext[8 - k:8 - k + GDN_C, :] * cw[CONV_K - 1 - k:CONV_K - k, :]
    return _silu(acc)


def _gdn_kernel(nq_ref, nk_ref, nv_ref, ng_ref, tail_ref, arow_ref, cwq_ref, cwk_ref, cwv_ref,
                alog_ref, dtb_ref, gn_ref, o_ref, gcr_ref):
    c = GDN_C
    n_chunks = nq_ref.shape[3] // c
    ri = lax.broadcasted_iota(jnp.int32, (c, c), 0)
    ci = lax.broadcasted_iota(jnp.int32, (c, c), 1)
    tril = ci <= ri
    strict = ci < ri
    tril_b = _ones_where(tril)
    triu_b = _ones_where(ri <= ci)
    gn = gn_ref[...]

    for hh in range(GDN_HEADS_PER_STEP):
        coef = -jnp.exp(alog_ref[hh])
        g_rows = coef[:, :1] * _softplus(arow_ref[hh, 0] + dtb_ref[hh][:, :1])
        g_rows = jnp.concatenate([g_rows, jnp.zeros((16 - n_chunks, c), F32)], axis=0)
        gcr_ref[hh] = _dot_exact_lhs(g_rows, triu_b)

    heads = range(GDN_HEADS_PER_STEP)

    def each(fn, *per_head):
        return [fn(*args) for args in zip(*per_head)]

    def chunk(n, states):
        r0 = pl.multiple_of(n * c, c)
        first = n == 0
        lanes = [slice(hh * HEAD_LANES, (hh + 1) * HEAD_LANES) for hh in heads]
        coef = [-jnp.exp(alog_ref[hh]) for hh in heads]
        dtb = [dtb_ref[hh] for hh in heads]
        sel_r = lax.broadcasted_iota(jnp.int32, (HEAD_LANES, 2 * HEAD_LANES), 0)
        sel_c = lax.broadcasted_iota(jnp.int32, (HEAD_LANES, 2 * HEAD_LANES), 1)
        hid = [pl.program_id(1) * GDN_HEADS_PER_STEP + hh for hh in heads]
        sel = [_ones_where(sel_r == jnp.where(sel_c < HEAD_LANES, h, DN_HEADS + h)) for h in hid]

        cq = [_causal_conv_silu(nq_ref, hh, r0, first, cwq_ref[:, lanes[hh]]) for hh in heads]
        ck = [_causal_conv_silu(nk_ref, hh, r0, first, cwk_ref[:, lanes[hh]]) for hh in heads]
        cv = [_causal_conv_silu(nv_ref, hh, r0, first, cwv_ref[:, lanes[hh]]) for hh in heads]
        qn = each(lambda a: (a * lax.rsqrt(jnp.sum(a * a, axis=-1, keepdims=True) + EPS))
                  * (DN_HEAD_DIM ** -0.5), cq)
        kn = each(lambda a: a * lax.rsqrt(jnp.sum(a * a, axis=-1, keepdims=True) + EPS), ck)

        tail_blk = tail_ref[pl.ds(r0, c), :]
        ba = each(lambda s_: _dot_exact_lhs(tail_blk, s_), sel)
        beta = each(lambda a: jax.nn.sigmoid(a[:, :HEAD_LANES]), ba)
        g_col = each(lambda a, cf, db: cf * _softplus(a[:, HEAD_LANES:] + db), ba, coef, dtb)
        gc = each(lambda g_: _dot_exact_rhs(tril_b, g_), g_col)
        gl = each(lambda g_: g_[c - 1:c, :], gc)
        eg = each(jnp.exp, gc)
        g_row = [gcr_ref[hh, pl.ds(n, 1), :] for hh in heads]

        kb = each(jnp.multiply, kn, beta)
        vb = each(jnp.multiply, cv, beta)
        kkqk = each(lambda kb_, qn_, kn_: _dot(jnp.concatenate([kb_, qn_], axis=0).astype(BF16),
                                               kn_.astype(BF16), NT), kb, qn, kn)
        decay = each(lambda g_, gr: jnp.exp(jnp.where(tril, jnp.concatenate([g_, g_], axis=1) - gr, -jnp.inf)),
                     gc, g_row)
        lmat = each(lambda kq, dc: jnp.where(strict, kq[:c] * dc, 0.0), kkqk, decay)
        attn = each(lambda kq, dc: kq[c:] * dc, kkqk, decay)

        same_base = (ri // GDN_BASE) == (ci // GDN_BASE)
        eye = jnp.where(ri == ci, 1.0, 0.0)
        p = each(lambda l_: jnp.where(same_base, l_, 0.0), lmat)
        tinv = each(lambda p_: eye - p_, p)
        for _ in range(int(math.log2(GDN_BASE)) - 1):
            p = each(lambda p_: _mm(p_, p_), p)
            tinv = each(lambda t_, p_: t_ + _mm(t_, p_), tinv, p)
        blk = GDN_BASE
        while blk < c:
            off_mask = (ri // (2 * blk) == ci // (2 * blk)) & (ri // blk != ci // blk)
            off = each(lambda l_: jnp.where(off_mask, l_, 0.0), lmat)
            half = each(_mm, tinv, off)
            tinv = each(lambda t_, h_: t_ - _mm(h_, t_), tinv, half)
            blk *= 2
        x = each(lambda t_, vb_, kb_, eg_: _mm(t_, jnp.concatenate([vb_, kb_ * eg_], axis=1)),
                 tinv, vb, kb, eg)

        sb = each(lambda s_: s_.astype(BF16), states)
        v_new = each(lambda x_, s_: x_[:, :HEAD_LANES] - _dot(x_[:, HEAD_LANES:].astype(BF16), s_), x, sb)
        vnb = each(lambda v_: v_.astype(BF16), v_new)
        out = each(lambda qn_, eg_, s_, at_, vn_: _dot((qn_ * eg_).astype(BF16), s_) + _dot(at_.astype(BF16), vn_),
                   qn, eg, sb, attn, vnb)
        k_tail = each(lambda kn_, gl_, g_: (kn_ * jnp.exp(gl_ - g_)).astype(BF16), kn, gl, gc)
        new_states = each(lambda st, gl_, kt, vn_: st * jnp.exp(gl_) + _dot(kt, vn_, ((0,), (0,))),
                          states, gl, k_tail, vnb)

        for hh in heads:
            o = out[hh]
            ms = jnp.mean(o * o, axis=-1, keepdims=True)
            y = (o * lax.rsqrt(ms + EPS)) * gn
            gate = ng_ref[0, 0, hh, pl.ds(r0, c), :]
            o_ref[0, pl.ds(r0, c), lanes[hh]] = (y * _silu(gate)).astype(o_ref.dtype)
        return tuple(new_states)

    state0 = tuple(jnp.zeros((DN_HEAD_DIM, DN_HEAD_DIM), F32) for _ in heads)
    lax.fori_loop(0, n_chunks, chunk, state0)


def _gated_deltanet(proj_main, tail, tail_t, conv_w, a_log_b, dt_bias_b, g_gnorm, batch, seq):
    c = GDN_C
    hps = GDN_HEADS_PER_STEP
    n_chunks = seq // c
    assert n_chunks <= 16 and DN_HEADS % hps == 0
    arow = tail_t.reshape(16, batch, n_chunks, c)
    head_spec = lambda g: pl.BlockSpec((1, 1, hps, seq, HEAD_LANES), lambda b, h: (g, b, h, 0, 0))
    cw_spec = lambda g: pl.BlockSpec((CONV_K, hps * HEAD_LANES), lambda b, h: (0, g * (DN_HEADS // hps) + h))
    vec_spec = pl.BlockSpec((hps, 1, HEAD_LANES), lambda b, h: (h, 0, 0))
    return pl.pallas_call(
        _gdn_kernel,
        grid=(batch, DN_HEADS // hps),
        in_specs=[
            head_spec(3), head_spec(4), head_spec(5), head_spec(6),
            pl.BlockSpec((seq, HEAD_LANES), lambda b, h: (b, 0)),
            pl.BlockSpec((hps, 1, n_chunks, c), lambda b, h: (DN_HEADS // hps + h, b, 0, 0)),
            cw_spec(0), cw_spec(1), cw_spec(2),
            vec_spec, vec_spec,
            pl.BlockSpec((1, DN_HEAD_DIM), lambda b, h: (0, 0)),
        ],
        out_specs=pl.BlockSpec((1, seq, hps * HEAD_LANES), lambda b, h: (b, 0, h)),
        out_shape=jax.ShapeDtypeStruct((batch, seq, DN_HEADS * DN_HEAD_DIM), BF16),
        scratch_shapes=[pltpu.VMEM((hps, 16, c), F32)],
        compiler_params=_cparams(("arbitrary", "arbitrary")),
        name="gated_deltanet",
    )(proj_main, proj_main, proj_main, proj_main, tail, arow, conv_w, conv_w, conv_w,
      a_log_b, dt_bias_b, g_gnorm)


OUT_TM = 512
OUT_TK = 512
EPI_ROWS = 256


def _pack_bf16_halves(x):
    n = x.shape[1] // 2
    bits = pltpu.bitcast(x.astype(BF16).astype(F32), jnp.int32)
    lo = lax.shift_right_logical(bits[:, :n], 16)
    hi = jnp.bitwise_and(bits[:, n:], jnp.int32(-65536))
    return jnp.bitwise_or(hi, lo)


def _unpack_bf16_halves(w):
    lo = pltpu.bitcast(lax.shift_left(w, 16), F32).astype(BF16)
    hi = pltpu.bitcast(jnp.bitwise_and(w, jnp.int32(-65536)), F32).astype(BF16)
    return lo, hi


def _dot_f32(a, b_parts):
    a_hi, a_mid, a_lo = _split3(a)
    b_hi, b_mid, b_lo = b_parts
    acc = _dot(a_hi, b_hi)
    acc += _dot(a_hi, b_mid) + _dot(a_mid, b_hi)
    acc += _dot(a_hi, b_lo) + _dot(a_mid, b_mid) + _dot(a_lo, b_hi)
    return acc


def _outproj_kernel(od_ref, on_ref, w_ref, x_ref, g1_ref, gf_ref, sc_ref, sh_ref, wr_ref, br_ref,
                    x1_ref, hn2_ref, idx_ref, gate_ref, rank_ref, cnt_ref, acc_ref, mh_ref, carry_ref):
    i = pl.program_id(0)
    k = pl.program_id(1)
    nk = pl.num_programs(1)
    tm = OUT_TM

    @pl.when((i == 0) & (k == 0))
    def _():
        carry_ref[...] = jnp.zeros_like(carry_ref)

    lhs = jnp.where(k < nk // 2, od_ref[...], on_ref[...])
    part = _dot(lhs, w_ref[...])

    @pl.when(k == 0)
    def _():
        acc_ref[...] = part

    @pl.when(k > 0)
    def _():
        acc_ref[...] += part

    @pl.when(k == nk - 1)
    def _():
        g1 = g1_ref[0]
        gf = gf_ref[...]
        sc = sc_ref[0]
        sh = sh_ref[0]
        wr_parts = _split3(wr_ref[...])
        br = br_ref[...]
        lane = lax.broadcasted_iota(jnp.int32, (EPI_ROWS, HEAD_LANES), 1)
        for r in range(0, tm, EPI_ROWS):
            rows = slice(r, r + EPI_ROWS)
            x1 = x_ref[rows, :] + g1 * acc_ref[rows, :]
            x1_ref[rows, :] = x1
            hn2 = _norm_modulate(x1, gf, sc, sh)
            hn2_ref[rows, :] = _pack_bf16_halves(hn2)
            logits = _dot_f32(hn2, wr_parts) + br
            l = jnp.where(lane < N_EXPERTS, logits, -jnp.inf)
            vals, idxs = [], []
            for _ in range(TOP_K):
                m = jnp.max(l, axis=-1, keepdims=True)
                ix = jnp.min(jnp.where(l == m, lane, HEAD_LANES), axis=-1, keepdims=True)
                vals.append(m)
                idxs.append(ix)
                l = jnp.where(lane == ix, -jnp.inf, l)
            es = [jnp.exp(v - vals[0]) for v in vals]
            den = (es[0] + es[1]) + (es[2] + es[3])
            idx_o = jnp.zeros((EPI_ROWS, HEAD_LANES), jnp.int32)
            gate_o = jnp.zeros((EPI_ROWS, HEAD_LANES), F32)
            hot = jnp.zeros((EPI_ROWS, HEAD_LANES), F32)
            for kk in range(TOP_K):
                idx_o = jnp.where(lane == kk, idxs[kk], idx_o)
                gate_o = jnp.where(lane == kk, es[kk] / den, gate_o)
                hot = jnp.where(lane == idxs[kk], 1.0, hot)
            idx_ref[rows, :] = idx_o
            gate_ref[rows, :] = gate_o
            mh_ref[rows, :] = hot.astype(BF16)

        ri = lax.broadcasted_iota(jnp.int32, (tm, tm), 0)
        ci = lax.broadcasted_iota(jnp.int32, (tm, tm), 1)
        mh = mh_ref[...]
        carry = carry_ref[0:1, :]
        cum = _dot(_ones_where(ci < ri), mh) + carry
        lane_t = lax.broadcasted_iota(jnp.int32, (tm, HEAD_LANES), 1)
        idx_t = idx_ref[...]
        rank_o = jnp.zeros((tm, HEAD_LANES), F32)
        for kk in range(TOP_K):
            ik = idx_t[:, kk:kk + 1]
            rk = jnp.sum(jnp.where(lane_t == ik, cum, 0.0), axis=-1, keepdims=True)
            rank_o = jnp.where(lane_t == kk, rk, rank_o)
        rank_ref[...] = rank_o.astype(jnp.int32)
        total = carry + jnp.sum(mh.astype(F32), axis=0, keepdims=True)
        carry_ref[...] = jnp.broadcast_to(total, carry_ref.shape)
        cnt_ref[...] = jnp.broadcast_to(total, cnt_ref.shape).astype(jnp.int32)


def _outproj_router(o_diff, o_dn, w_out, x2, gate1, g_ffn, scale2, shift2, w_router_pad, b_router_pad, seq):
    n_tok, d = x2.shape
    tm, tk = OUT_TM, OUT_TK
    tiles_per_batch = seq // tm
    nk = d // tk
    half = nk // 2
    vec3 = pl.BlockSpec((1, 1, d), lambda i, k: (i // tiles_per_batch, 0, 0))
    row_out = lambda dt: jax.ShapeDtypeStruct((n_tok, HEAD_LANES), dt)
    return pl.pallas_call(
        _outproj_kernel,
        grid=(n_tok // tm, nk),
        in_specs=[
            pl.BlockSpec((tm, tk), lambda i, k: (i, jnp.minimum(k, half - 1))),
            pl.BlockSpec((tm, tk), lambda i, k: (i, jnp.maximum(k - half, 0))),
            pl.BlockSpec((tk, d), lambda i, k: (k, 0)),
            pl.BlockSpec((tm, d), lambda i, k: (i, 0)),
            vec3,
            pl.BlockSpec((1, d), lambda i, k: (0, 0)),
            vec3, vec3,
            pl.BlockSpec((d, HEAD_LANES), lambda i, k: (0, 0)),
            pl.BlockSpec((1, HEAD_LANES), lambda i, k: (0, 0)),
        ],
        out_specs=[
            pl.BlockSpec((tm, d), lambda i, k: (i, 0)),
            pl.BlockSpec((tm, d // 2), lambda i, k: (i, 0)),
            pl.BlockSpec((tm, HEAD_LANES), lambda i, k: (i, 0)),
            pl.BlockSpec((tm, HEAD_LANES), lambda i, k: (i, 0)),
            pl.BlockSpec((tm, HEAD_LANES), lambda i, k: (i, 0)),
            pl.BlockSpec((8, HEAD_LANES), lambda i, k: (0, 0)),
        ],
        out_shape=[
            jax.ShapeDtypeStruct((n_tok, d), F32),
            jax.ShapeDtypeStruct((n_tok, d // 2), jnp.int32),
            row_out(jnp.int32), row_out(F32), row_out(jnp.int32),
            jax.ShapeDtypeStruct((8, HEAD_LANES), jnp.int32),
        ],
        scratch_shapes=[pltpu.VMEM((tm, d), F32), pltpu.VMEM((tm, HEAD_LANES), BF16),
                        pltpu.VMEM((8, HEAD_LANES), F32)],
        compiler_params=_cparams(("arbitrary", "arbitrary")),
        name="outproj_router",
    )(o_diff, o_dn, w_out, x2, gate1, g_ffn, scale2, shift2, w_router_pad, b_router_pad)


MOE_TM = 2048
MOE_SUB = 256
MOE_CHUNK = 512
MOE_TF = 512
MOE_TN = 512
DSP_TM = 256


def _row_copy(src_hbm, row, dst_vmem, slot, sem):
    return pltpu.make_async_copy(src_hbm.at[pl.ds(row, 1)], dst_vmem.at[pl.ds(slot, 1)], sem)


def _dispatch_kernel(sbr_ref, sbm_ref, dest_ref, hn2_ref, xs_hbm, zero_ref, sem):
    i = pl.program_id(0)
    tm = DSP_TM

    @pl.when(i == 0)
    def _():
        zero_ref[...] = jnp.zeros_like(zero_ref)
        n_sb = sbr_ref.shape[0]

        def pad_copies(s, fn):
            nv = sbr_ref[s]
            nv8 = (nv + 7) // 8 * 8
            for t in range(7):
                @pl.when(nv + t < nv8)
                def _():
                    fn(pltpu.make_async_copy(zero_ref.at[pl.ds(0, 1)],
                                             xs_hbm.at[pl.ds(s * MOE_TM + nv + t, 1)], sem))

            pad8 = (sbm_ref[s] - nv8) // 8
            bit = MOE_SUB // 16
            while bit >= 1:
                rows = 8 * bit
                start = pl.multiple_of(s * MOE_TM + nv8 + 8 * (pad8 & ~(2 * bit - 1)), 8)

                @pl.when((pad8 & bit) != 0)
                def _():
                    fn(pltpu.make_async_copy(zero_ref.at[pl.ds(0, rows)],
                                             xs_hbm.at[pl.ds(start, rows)], sem))

                bit //= 2
            return 0

        lax.fori_loop(0, n_sb, lambda s, c: c + pad_copies(s, lambda cp: cp.start()), 0)
        lax.fori_loop(0, n_sb, lambda s, c: c + pad_copies(s, lambda cp: cp.wait()), 0)

    def issue(r, carry):
        for kk in range(TOP_K):
            pltpu.make_async_copy(hn2_ref.at[pl.ds(r, 1)],
                                  xs_hbm.at[pl.ds(dest_ref[0, 0, kk * tm + r], 1)], sem).start(priority=kk % 2)
        return carry

    lax.fori_loop(0, tm, issue, 0)

    def wait(r, carry):
        for kk in range(TOP_K):
            pltpu.make_async_copy(hn2_ref.at[pl.ds(0, 1)], xs_hbm.at[pl.ds(0, 1)], sem).wait()
        return carry

    lax.fori_loop(0, tm, wait, 0)


def _dispatch(sb_rows, sb_m, dest_tiles, hn2, n_rows):
    n_tok, d = hn2.shape
    tm = DSP_TM
    grid_spec = pltpu.PrefetchScalarGridSpec(
        num_scalar_prefetch=2,
        grid=(n_tok // tm,),
        in_specs=[
            pl.BlockSpec((1, 1, tm * TOP_K), lambda i, sbr, sbm: (i, 0, 0), memory_space=pltpu.SMEM),
            pl.BlockSpec((tm, d), lambda i, sbr, sbm: (i, 0)),
        ],
        out_specs=pl.BlockSpec(memory_space=pl.ANY),
        scratch_shapes=[pltpu.VMEM((MOE_SUB // 2, d), hn2.dtype), pltpu.SemaphoreType.DMA(())],
    )
    return pl.pallas_call(
        _dispatch_kernel,
        grid_spec=grid_spec,
        out_shape=jax.ShapeDtypeStruct((n_rows, d), hn2.dtype),
        compiler_params=_cparams(("arbitrary",)),
        name="moe_dispatch",
    )(sb_rows, sb_m, dest_tiles, hn2)


def _for_row_chunks(m, body):
    n_full = m // MOE_CHUNK

    def step(c, carry):
        body(pl.multiple_of(c * MOE_CHUNK, MOE_CHUNK), MOE_CHUNK)
        return carry

    lax.fori_loop(0, n_full, step, 0)
    assert MOE_CHUNK == 2 * MOE_SUB

    @pl.when(m - n_full * MOE_CHUNK > 0)
    def _():
        body(pl.multiple_of(n_full * MOE_CHUNK, MOE_CHUNK), MOE_SUB)


def _moe_up_kernel(sbe_ref, sbm_ref, nused_ref, x_ref, wg_ref, wu_ref, bg_ref, bu_ref, h_ref,
                   xb_ref, wgb_ref, wub_ref):
    s = pl.program_id(0)
    f = pl.program_id(1)
    m = sbm_ref[s]
    half = x_ref.shape[1]

    @pl.when(f == 0)
    def _():
        def unpack(r0, rows):
            lo, hi = _unpack_bf16_halves(x_ref[pl.ds(r0, rows), :])
            xb_ref[pl.ds(r0, rows), 0:half] = lo
            xb_ref[pl.ds(r0, rows), half:2 * half] = hi

        _for_row_chunks(m, unpack)

    @pl.when(m > 0)
    def _():
        wgb_ref[...] = wg_ref[0].astype(BF16)
        wub_ref[...] = wu_ref[0].astype(BF16)

    def compute(r0, rows):
        xq = xb_ref[pl.ds(r0, rows), :]
        gate = jnp.minimum(_dot(xq, wgb_ref[...]) + bg_ref[0], SWIGLU_LIMIT)
        up = jnp.clip(_dot(xq, wub_ref[...]) + bu_ref[0], -SWIGLU_LIMIT, SWIGLU_LIMIT)
        glu = gate * jax.nn.sigmoid(gate * SWIGLU_ALPHA)
        h_ref[pl.ds(r0, rows), :] = ((up + 1.0) * glu).astype(BF16)

    _for_row_chunks(m, compute)


def _moe_up(sb_expert, sb_m, n_used, xs, w_gate_up, b_gate_up):
    n_sb = sb_expert.shape[0]
    d = xs.shape[1]
    n_exp, dm, f2 = w_gate_up.shape
    assert dm == 2 * d
    f_dim = f2 // 2
    nf = f_dim // MOE_TF

    def f_eff(s, f, nu):
        return jnp.where(s < nu[0], f, nf - 1)

    def s_eff(s, nu):
        return jnp.minimum(s, nu[0] - 1)

    grid_spec = pltpu.PrefetchScalarGridSpec(
        num_scalar_prefetch=3,
        grid=(n_sb, nf),
        in_specs=[
            pl.BlockSpec((MOE_TM, d), lambda s, f, sbe, sbm, nu: (s_eff(s, nu), 0)),
            pl.BlockSpec((1, dm, MOE_TF), lambda s, f, sbe, sbm, nu: (sbe[s], 0, f_eff(s, f, nu))),
            pl.BlockSpec((1, dm, MOE_TF), lambda s, f, sbe, sbm, nu: (sbe[s], 0, nf + f_eff(s, f, nu))),
            pl.BlockSpec((1, 1, MOE_TF), lambda s, f, sbe, sbm, nu: (sbe[s], 0, f_eff(s, f, nu))),
            pl.BlockSpec((1, 1, MOE_TF), lambda s, f, sbe, sbm, nu: (sbe[s], 0, nf + f_eff(s, f, nu))),
        ],
        out_specs=pl.BlockSpec((MOE_TM, MOE_TF), lambda s, f, sbe, sbm, nu: (s_eff(s, nu), f_eff(s, f, nu))),
        scratch_shapes=[pltpu.VMEM((MOE_TM, 2 * d), BF16), pltpu.VMEM((2 * d, MOE_TF), BF16),
                        pltpu.VMEM((2 * d, MOE_TF), BF16)],
    )
    b3 = b_gate_up.reshape(n_exp, 1, f2)
    return pl.pallas_call(
        _moe_up_kernel,
        grid_spec=grid_spec,
        out_shape=jax.ShapeDtypeStruct((n_sb * MOE_TM, f_dim), BF16),
        compiler_params=_cparams(("arbitrary", "arbitrary")),
        name="moe_up",
    )(sb_expert, sb_m, n_used, xs, w_gate_up, w_gate_up, b3, b3)


def _moe_down_kernel(sbe_ref, sbm_ref, nused_ref, h_ref, wd_ref, bd_ref, o_ref, wdb_ref):
    s = pl.program_id(0)
    m = sbm_ref[s]

    @pl.when(m > 0)
    def _():
        wdb_ref[...] = wd_ref[0].astype(BF16)

    def compute(r0, rows):
        o_ref[pl.ds(r0, rows), :] = _dot(h_ref[pl.ds(r0, rows), :], wdb_ref[...]) + bd_ref[0]

    _for_row_chunks(m, compute)


def _moe_down(sb_expert, sb_m, n_used, h, w_down, b_down):
    n_sb = sb_expert.shape[0]
    n_exp, f_dim, d = w_down.shape
    nn = d // MOE_TN

    def n_eff(s, n, nu):
        return jnp.where(s < nu[0], n, nn - 1)

    def s_eff(s, nu):
        return jnp.minimum(s, nu[0] - 1)

    grid_spec = pltpu.PrefetchScalarGridSpec(
        num_scalar_prefetch=3,
        grid=(n_sb, nn),
        in_specs=[
            pl.BlockSpec((MOE_TM, f_dim), lambda s, n, sbe, sbm, nu: (s_eff(s, nu), 0)),
            pl.BlockSpec((1, f_dim, MOE_TN), lambda s, n, sbe, sbm, nu: (sbe[s], 0, n_eff(s, n, nu))),
            pl.BlockSpec((1, 1, MOE_TN), lambda s, n, sbe, sbm, nu: (sbe[s], 0, n_eff(s, n, nu))),
        ],
        out_specs=pl.BlockSpec((MOE_TM, MOE_TN), lambda s, n, sbe, sbm, nu: (s_eff(s, nu), n_eff(s, n, nu))),
        scratch_shapes=[pltpu.VMEM((f_dim, MOE_TN), BF16)],
    )
    return pl.pallas_call(
        _moe_down_kernel,
        grid_spec=grid_spec,
        out_shape=jax.ShapeDtypeStruct((n_sb * MOE_TM, d), F32),
        compiler_params=_cparams(("arbitrary", "arbitrary")),
        name="moe_down",
    )(sb_expert, sb_m, n_used, h, w_down, b_down.reshape(n_exp, 1, d))


CMB_TM = 256


def _combine_kernel(dest_ref, ys_hbm, x1_ref, gate_ref, g2_ref, gfin_ref, o_ref, buf_ref, sem):
    tm = CMB_TM

    def issue(r, carry):
        for kk in range(TOP_K):
            _row_copy(ys_hbm, dest_ref[0, 0, kk * tm + r], buf_ref.at[kk], r, sem).start(priority=kk % 2)
        return carry

    lax.fori_loop(0, tm, issue, 0)

    def wait(r, carry):
        for kk in range(TOP_K):
            _row_copy(ys_hbm, 0, buf_ref.at[kk], 0, sem).wait()
        return carry

    lax.fori_loop(0, tm, wait, 0)

    gates = gate_ref[...]
    y = gates[:, 0:1] * buf_ref[0]
    for kk in range(1, TOP_K):
        y = y + gates[:, kk:kk + 1] * buf_ref[kk]
    x2 = x1_ref[...] + g2_ref[0] * y
    ms = jnp.mean(x2 * x2, axis=-1, keepdims=True)
    o_ref[...] = (x2 * lax.rsqrt(ms + EPS)) * gfin_ref[...]


def _combine(dest_tiles, ys, x1, gates, gate2, g_final, seq):
    n_tok, d = x1.shape
    tm = CMB_TM
    tiles_per_batch = seq // tm
    return pl.pallas_call(
        _combine_kernel,
        grid=(n_tok // tm,),
        in_specs=[
            pl.BlockSpec((1, 1, tm * TOP_K), lambda i: (i, 0, 0), memory_space=pltpu.SMEM),
            pl.BlockSpec(memory_space=pl.ANY),
            pl.BlockSpec((tm, d), lambda i: (i, 0)),
            pl.BlockSpec((tm, HEAD_LANES), lambda i: (i, 0)),
            pl.BlockSpec((1, 1, d), lambda i: (i // tiles_per_batch, 0, 0)),
            pl.BlockSpec((1, d), lambda i: (0, 0)),
        ],
        out_specs=pl.BlockSpec((tm, d), lambda i: (i, 0)),
        out_shape=jax.ShapeDtypeStruct((n_tok, d), F32),
        scratch_shapes=[pltpu.VMEM((TOP_K, tm, d), F32), pltpu.SemaphoreType.DMA(())],
        compiler_params=_cparams(("arbitrary",)),
        name="moe_combine_norm",
    )(dest_tiles, ys, x1, gates, gate2, g_final)


def _route_tables(top_idx, rank, counts, n_tok):
    n_sb = n_tok * TOP_K // MOE_TM + N_EXPERTS
    max_sb_per_e = n_tok // MOE_TM
    cnt = counts[0, :N_EXPERTS]
    sb_per_e = (cnt + MOE_TM - 1) // MOE_TM
    rows_per_sb = (cnt + jnp.maximum(sb_per_e, 1) - 1) // jnp.maximum(sb_per_e, 1)
    sb_end = jnp.cumsum(sb_per_e)
    sb_start = sb_end - sb_per_e
    n_used = sb_end[-1]
    s_ids = jnp.arange(n_sb, dtype=jnp.int32)
    e_ids = jnp.arange(N_EXPERTS, dtype=jnp.int32)
    s_clamped = jnp.minimum(s_ids, n_used - 1)
    e_of_s = jnp.sum((sb_end[None, :] <= s_clamped[:, None]).astype(jnp.int32), axis=1)
    e_of_s = jnp.minimum(e_of_s, N_EXPERTS - 1).astype(jnp.int32)
    onehot_s = (e_of_s[:, None] == e_ids[None, :]).astype(jnp.int32)
    j_of_s = s_ids - jnp.sum(onehot_s * sb_start[None, :], axis=1)
    per_s = jnp.sum(onehot_s * rows_per_sb[None, :], axis=1)
    cnt_s = jnp.sum(onehot_s * cnt[None, :], axis=1)
    sb_rows = jnp.where(s_ids < n_used, jnp.clip(cnt_s - j_of_s * per_s, 0, per_s), 0).astype(jnp.int32)
    sb_m = ((sb_rows + MOE_SUB - 1) // MOE_SUB * MOE_SUB).astype(jnp.int32)
    idx_t = top_idx[:, :TOP_K].T
    rank_t = rank[:, :TOP_K].T
    onehot_a = idx_t[None, :, :] == e_ids[:, None, None]
    start_a = jnp.sum(jnp.where(onehot_a, sb_start[:, None, None], 0), axis=0)
    per_a = jnp.sum(jnp.where(onehot_a, rows_per_sb[:, None, None], 0), axis=0)
    sb_in_e = sum((rank_t >= j * per_a).astype(jnp.int32) for j in range(1, max_sb_per_e))
    dest_t = ((start_a + sb_in_e) * MOE_TM + (rank_t - sb_in_e * per_a)).astype(jnp.int32)
    return e_of_s, sb_rows, sb_m, n_used.reshape(1).astype(jnp.int32), dest_t, n_sb


def _moe_block(hn2_packed, top_idx, rank, counts, x1, gates, gate2, w_gate_up, b_gate_up, w_down, b_down,
               g_final, seq):
    n_tok = x1.shape[0]
    e_of_s, sb_rows, sb_m, n_used1, dest_t, n_sb = _route_tables(top_idx, rank, counts, n_tok)

    def dest_tiles(tm):
        return dest_t.reshape(TOP_K, n_tok // tm, tm).transpose(1, 0, 2).reshape(n_tok // tm, 1, TOP_K * tm)

    xs = _dispatch(sb_rows, sb_m, dest_tiles(DSP_TM), hn2_packed, n_sb * MOE_TM)
    hmid = _moe_up(e_of_s, sb_m, n_used1, xs, w_gate_up, b_gate_up)
    ys = _moe_down(e_of_s, sb_m, n_used1, hmid, w_down, b_down)
    return _combine(dest_tiles(CMB_TM), ys, x1, gates, gate2, g_final[None, :], seq)


def kernel(x, c, w_ada, b_ada, g_norm_mix, g_norm_ffn, g_norm_final, w_in, conv_w, rel_bias,
           lambda_q1, lambda_k1, lambda_q2, lambda_k2, g_subln, a_log, dt_bias, g_gnorm, w_out,
           w_router, b_router, w_gate_up, b_gate_up, w_down, b_down):
    batch, seq, d = x.shape
    n_tok = batch * seq
    depth = w_ada.shape[0]
    assert depth == 1 and d == D_MODEL
    x2 = x.reshape(n_tok, d)
    bias_tiles = _bias_tiles(rel_bias)
    for l in range(depth):
        c_pad = jnp.zeros((16, d), F32).at[:batch].set(c)
        mod = _adaln(c_pad, w_ada[l], b_ada[l][None, :])[:batch]
        shift1, scale1, gate1, shift2, scale2, gate2 = [m.reshape(batch, 1, d) for m in jnp.split(mod, 6, axis=-1)]

        w_tail = jnp.pad(w_in[l, :, MAIN_COLS:], ((0, 0), (0, HEAD_LANES - 2 * DN_HEADS)))
        w_tail_t = w_in[l, :, MAIN_COLS:].T
        proj_main, tail, tail_t = _inproj(x2, g_norm_mix[l][None, :], scale1, shift1,
                                          w_in[l, :, :MAIN_COLS].astype(BF16), w_tail, w_tail_t, batch, seq)
        o_diff = _diff_attention(proj_main, bias_tiles, lambda_q1[l][None, :], lambda_k1[l][None, :],
                                 lambda_q2[l][None, :], lambda_k2[l][None, :], g_subln[l][None, :], batch, seq)
        a_log_b = jnp.broadcast_to(a_log[l][:, None, None], (DN_HEADS, 1, HEAD_LANES))
        dt_bias_b = jnp.broadcast_to(dt_bias[l][:, None, None], (DN_HEADS, 1, HEAD_LANES))
        o_dn = _gated_deltanet(proj_main, tail, tail_t, conv_w[l], a_log_b, dt_bias_b, g_gnorm[l][None, :],
                               batch, seq)

        w_router_pad = jnp.pad(w_router[l], ((0, 0), (0, HEAD_LANES - N_EXPERTS)))
        b_router_pad = jnp.pad(b_router[l], (0, HEAD_LANES - N_EXPERTS))[None, :]
        x1, hn2, top_idx, gates, rank, counts = _outproj_router(
            o_diff.reshape(n_tok, -1), o_dn.reshape(n_tok, -1), w_out[l].astype(BF16), x2, gate1, g_norm_ffn[l][None, :],
            scale2, shift2, w_router_pad, b_router_pad, seq)

        x2 = _moe_block(hn2, top_idx, rank, counts, x1, gates, gate2, w_gate_up[l], b_gate_up[l], w_down[l],
                        b_down[l], g_norm_final, seq)
    return x2.reshape(batch, seq, d)
```

```python
import functools
import math

import jax
import jax.numpy as jnp
import numpy as np
from jax import lax
from jax.experimental import pallas as pl
from jax.experimental.pallas import tpu as pltpu

F32 = jnp.float32
BF16 = jnp.bfloat16

D_MODEL = 2048
DIFF_HEADS = 8
DIFF_HEAD_DIM = 64
DIFF_V_DIM = 128
DN_HEADS = 8
DN_HEAD_DIM = 128
HEAD_LANES = 128
CONV_K = 4
N_BUCKETS = 32
MAX_DISTANCE = 128
N_EXPERTS = 32
TOP_K = 4
D_EXPERT = 2048
SWIGLU_LIMIT = 7.0
SWIGLU_ALPHA = 1.702
EPS = 1e-6
LAMBDA_INIT = 0.8 - 0.6 * math.exp(-0.3 * 0)
N_MAIN_GROUPS = 7
MAIN_COLS = N_MAIN_GROUPS * 1024

V7X_VMEM_BYTES = 64 * 1024 * 1024
VMEM_LIMIT = 56 * 1024 * 1024

NEG_BIG = -1e30


def _cparams(sem):
    return pltpu.CompilerParams(dimension_semantics=sem, vmem_limit_bytes=VMEM_LIMIT)


def _split3(a):
    hi = a.astype(BF16)
    r1 = a - hi.astype(F32)
    mid = r1.astype(BF16)
    lo = (r1 - mid.astype(F32)).astype(BF16)
    return hi, mid, lo


def _dot(a, b, dims=None):
    if dims is None:
        return jnp.dot(a, b, preferred_element_type=F32)
    return lax.dot_general(a, b, (dims, ((), ())), preferred_element_type=F32)


def _dot_exact_lhs(a, b_bf16, dims=None):
    hi, mid, lo = _split3(a)
    return (_dot(hi, b_bf16, dims) + _dot(mid, b_bf16, dims)) + _dot(lo, b_bf16, dims)


def _dot_exact_rhs(a_bf16, b, dims=None):
    hi, mid, lo = _split3(b)
    return (_dot(a_bf16, hi, dims) + _dot(a_bf16, mid, dims)) + _dot(a_bf16, lo, dims)


NT = ((1,), (1,))


def _ones_where(mask):
    return jnp.where(mask, 1.0, 0.0).astype(BF16)


ADA_TN = 1024


def _adaln_kernel(c_ref, w_ref, b_ref, o_ref):
    c = c_ref[...]
    sc = c * jax.nn.sigmoid(c)
    a_hi, a_mid, a_lo = _split3(sc)
    w_hi, w_mid, w_lo = _split3(w_ref[...])
    acc = _dot(a_hi, w_hi)
    acc += _dot(a_hi, w_mid) + _dot(a_mid, w_hi)
    acc += _dot(a_hi, w_lo) + _dot(a_mid, w_mid) + _dot(a_lo, w_hi)
    o_ref[...] = acc + b_ref[...]


def _adaln(c_pad, w, b):
    n = w.shape[1]
    return pl.pallas_call(
        _adaln_kernel,
        grid=(n // ADA_TN,),
        in_specs=[
            pl.BlockSpec(c_pad.shape, lambda j: (0, 0)),
            pl.BlockSpec((w.shape[0], ADA_TN), lambda j: (0, j)),
            pl.BlockSpec((1, ADA_TN), lambda j: (0, j)),
        ],
        out_specs=pl.BlockSpec((c_pad.shape[0], ADA_TN), lambda j: (0, j)),
        out_shape=jax.ShapeDtypeStruct((c_pad.shape[0], n), F32),
        compiler_params=_cparams(("arbitrary",)),
        name="adaln_mod",
    )(c_pad, w, b)


INPROJ_TM = 1024
INPROJ_TN = 1024
NORM_ROWS = 256


def _norm_modulate(x, g, scale, shift):
    ms = jnp.mean(x * x, axis=-1, keepdims=True)
    y = (x * lax.rsqrt(ms + EPS)) * g
    return y * (1.0 + scale) + shift


def _inproj_kernel(x_ref, g_ref, sc_ref, sh_ref, w_ref, wt_ref, wtt_ref, o_ref, t_ref, tt_ref, hn_ref):
    j = pl.program_id(1)

    @pl.when(j == 0)
    def _():
        g = g_ref[...]
        sc = sc_ref[0]
        sh = sh_ref[0]
        for r in range(0, INPROJ_TM, NORM_ROWS):
            hn = _norm_modulate(x_ref[r:r + NORM_ROWS, :], g, sc, sh)
            hn_ref[r:r + NORM_ROWS, :] = hn.astype(BF16)
        hnb = hn_ref[...]
        t_ref[...] = _dot(hnb, wt_ref[...].astype(BF16))
        tt_ref[...] = _dot(wtt_ref[...].astype(BF16), hnb, NT)

    acc = _dot(hn_ref[...], w_ref[...])
    for hh in range(INPROJ_TN // HEAD_LANES):
        o_ref[0, 0, hh] = acc[:, hh * HEAD_LANES:(hh + 1) * HEAD_LANES]


def _inproj(x2, g_mix, scale1, shift1, w_in, w_tail, w_tail_t, batch, seq):
    n_tok, d = x2.shape
    tiles_per_batch = seq // INPROJ_TM
    heads_per_tile = INPROJ_TN // HEAD_LANES
    tiles_per_group = 1024 // INPROJ_TN
    grid = (n_tok // INPROJ_TM, MAIN_COLS // INPROJ_TN)
    return pl.pallas_call(
        _inproj_kernel,
        grid=grid,
        in_specs=[
            pl.BlockSpec((INPROJ_TM, d), lambda i, j: (i, 0)),
            pl.BlockSpec((1, d), lambda i, j: (0, 0)),
            pl.BlockSpec((1, 1, d), lambda i, j: (i // tiles_per_batch, 0, 0)),
            pl.BlockSpec((1, 1, d), lambda i, j: (i // tiles_per_batch, 0, 0)),
            pl.BlockSpec((d, INPROJ_TN), lambda i, j: (0, j)),
            pl.BlockSpec((d, HEAD_LANES), lambda i, j: (0, 0)),
            pl.BlockSpec((16, d), lambda i, j: (0, 0)),
        ],
        out_specs=[
            pl.BlockSpec((1, 1, heads_per_tile, INPROJ_TM, HEAD_LANES),
                         lambda i, j: (j // tiles_per_group, i // tiles_per_batch, j % tiles_per_group,
                                       i % tiles_per_batch, 0)),
            pl.BlockSpec((INPROJ_TM, HEAD_LANES), lambda i, j: (i, 0)),
            pl.BlockSpec((16, INPROJ_TM), lambda i, j: (0, i)),
        ],
        out_shape=[
            jax.ShapeDtypeStruct((N_MAIN_GROUPS, batch, 8, seq, HEAD_LANES), F32),
            jax.ShapeDtypeStruct((n_tok, HEAD_LANES), F32),
            jax.ShapeDtypeStruct((16, n_tok), F32),
        ],
        scratch_shapes=[pltpu.VMEM((INPROJ_TM, d), BF16)],
        compiler_params=_cparams(("arbitrary", "arbitrary")),
        name="norm_inproj",
    )(x2, g_mix, scale1, shift1, w_in, w_tail, w_tail_t)


ATT_T = 256
ATT_ROWS = 128


def _t5_bucket(n):
    n = jnp.maximum(n, 0)
    max_exact = N_BUCKETS // 2
    nf = jnp.maximum(n, max_exact).astype(F32)
    large = max_exact + (jnp.log(nf / max_exact) / math.log(MAX_DISTANCE / max_exact)
                         * (N_BUCKETS - max_exact)).astype(jnp.int32)
    large = jnp.minimum(large, N_BUCKETS - 1)
    return jnp.where(n < max_exact, n, large)


def _bias_tiles(rel_bias):
    assert ATT_T >= MAX_DISTANCE
    t = ATT_T
    table = rel_bias.astype(F32)
    n_heads = table.shape[1]
    tiles = []
    for d in range(2):
        dist = (d * t + t) - jnp.arange(2 * t, dtype=jnp.int32)
        w = jnp.where(dist[:, None] >= 0, table[_t5_bucket(dist)], F32(NEG_BIG)).T
        w_ext = jnp.concatenate([w, jnp.zeros((n_heads, 1), F32)], axis=1)
        skew = jnp.tile(w_ext, (1, t))[:, :t * 2 * t].reshape(n_heads, t, 2 * t)
        tiles.append(skew[:, :, t:])
    far = jnp.broadcast_to(table[N_BUCKETS - 1][:, None, None], (n_heads, t, t))
    return jnp.stack(tiles + [far], axis=1)


def _attn_kernel(q_ref, k_ref, v_ref, bias_ref, lq1_ref, lk1_ref, lq2_ref, lk2_ref, gs_ref,
                 o_ref, kb_ref, vb_ref, s0_ref, s1_ref, p_ref, m_ref, l_ref, acc_ref):
    i = pl.program_id(2)
    t = ATT_T
    rows2 = 2 * t

    @pl.when(i == 0)
    def _():
        kb_ref[...] = k_ref[0, 0, 0].astype(BF16)
        vb_ref[...] = v_ref[0, 0, 0].astype(BF16)

    q = q_ref[0, 0, 0] * (DIFF_HEAD_DIM ** -0.5)
    lane = lax.broadcasted_iota(jnp.int32, q.shape, 1)
    q1 = jnp.where(lane < DIFF_HEAD_DIM, q, 0.0)
    q2 = jnp.where(lane >= DIFF_HEAD_DIM, q, 0.0)
    q12 = jnp.concatenate([q1, q2], axis=0).astype(BF16)

    m_ref[...] = jnp.full(m_ref.shape, -jnp.inf, F32)
    l_ref[...] = jnp.zeros(l_ref.shape, F32)
    acc_ref[...] = jnp.zeros(acc_ref.shape, F32)

    def scores(j):
        return _dot(q12, kb_ref[pl.ds(pl.multiple_of(j * t, t), t), :], NT)

    def block(j, s_cur_ref, s_next_ref):
        s_next_ref[...] = scores(jnp.minimum(j + 1, i))
        d = jnp.minimum(i - j, 2)
        for r in range(0, rows2, ATT_ROWS):
            rows = slice(r, r + ATT_ROWS)
            brows = slice(r % t, r % t + ATT_ROWS)
            halves = [s_cur_ref[rows, c:c + HEAD_LANES] + bias_ref[0, d, brows, c:c + HEAD_LANES]
                      for c in range(0, t, HEAD_LANES)]
            m_old = m_ref[rows, :]
            m_new = jnp.maximum(m_old, jnp.max(functools.reduce(jnp.maximum, halves), axis=-1, keepdims=True))
            alpha = jnp.exp(m_old - m_new)
            ps = [jnp.exp(sh - m_new) for sh in halves]
            l_ref[rows, :] = alpha * l_ref[rows, :] + jnp.sum(functools.reduce(jnp.add, ps), axis=-1, keepdims=True)
            m_ref[rows, :] = m_new
            for c, ph in zip(range(0, t, HEAD_LANES), ps):
                p_ref[rows, c:c + HEAD_LANES] = ph.astype(BF16)
            acc_ref[rows, :] = alpha * acc_ref[rows, :]
        acc_ref[...] += _dot(p_ref[...], vb_ref[pl.ds(pl.multiple_of(j * t, t), t), :])

    s0_ref[...] = scores(0)

    def pair(pidx, carry):
        j0 = 2 * pidx
        block(j0, s0_ref, s1_ref)

        @pl.when(j0 + 1 <= i)
        def _():
            block(j0 + 1, s1_ref, s0_ref)

        return carry

    lax.fori_loop(0, (i + 2) // 2, pair, 0)

    lam = (jnp.exp(jnp.sum(lq1_ref[...] * lk1_ref[...], axis=-1, keepdims=True))
           - jnp.exp(jnp.sum(lq2_ref[...] * lk2_ref[...], axis=-1, keepdims=True)) + LAMBDA_INIT)
    o = acc_ref[0:t, :] / l_ref[0:t, :] - lam * (acc_ref[t:rows2, :] / l_ref[t:rows2, :])
    ms = jnp.mean(o * o, axis=-1, keepdims=True)
    y = (o * lax.rsqrt(ms + EPS)) * gs_ref[...]
    o_ref[0] = (y * (1.0 - LAMBDA_INIT)).astype(o_ref.dtype)


def _diff_attention(proj_main, bias_tiles, lq1, lk1, lq2, lk2, g_subln, batch, seq):
    t = ATT_T
    nq = seq // t
    lam_spec = pl.BlockSpec((1, DIFF_HEAD_DIM), lambda b, h, i: (0, 0))
    return pl.pallas_call(
        _attn_kernel,
        grid=(batch, DIFF_HEADS, nq),
        in_specs=[
            pl.BlockSpec((1, 1, 1, t, HEAD_LANES), lambda b, h, i: (0, b, h, i, 0)),
            pl.BlockSpec((1, 1, 1, seq, HEAD_LANES), lambda b, h, i: (1, b, h, 0, 0)),
            pl.BlockSpec((1, 1, 1, seq, HEAD_LANES), lambda b, h, i: (2, b, h, 0, 0)),
            pl.BlockSpec((1, 3, t, t), lambda b, h, i: (h, 0, 0, 0)),
            lam_spec, lam_spec, lam_spec, lam_spec,
            pl.BlockSpec((1, DIFF_V_DIM), lambda b, h, i: (0, 0)),
        ],
        out_specs=pl.BlockSpec((1, t, HEAD_LANES), lambda b, h, i: (b, i, h)),
        scratch_shapes=[
            pltpu.VMEM((seq, HEAD_LANES), BF16), pltpu.VMEM((seq, HEAD_LANES), BF16),
            pltpu.VMEM((2 * t, t), F32), pltpu.VMEM((2 * t, t), F32), pltpu.VMEM((2 * t, t), BF16),
            pltpu.VMEM((2 * t, HEAD_LANES), F32), pltpu.VMEM((2 * t, HEAD_LANES), F32),
            pltpu.VMEM((2 * t, DIFF_V_DIM), F32),
        ],
        out_shape=jax.ShapeDtypeStruct((batch, seq, DIFF_HEADS * DIFF_V_DIM), BF16),
        compiler_params=_cparams(("arbitrary", "arbitrary", "arbitrary")),
        name="diff_attention",
    )(proj_main, proj_main, proj_main, bias_tiles, lq1, lk1, lq2, lk2, g_subln)


GDN_C = 256
GDN_BASE = 16
GDN_HEADS_PER_STEP = 4


def _softplus(x):
    return jnp.maximum(x, 0.0) + jnp.log1p(jnp.exp(-jnp.abs(x)))


def _silu(x):
    return x * jax.nn.sigmoid(x)


def _mm(a, b):
    return _dot(a.astype(BF16), b.astype(BF16))


def _causal_conv_silu(ref, hh, r0, first, cw):
    cur = ref[0, 0, hh, pl.ds(r0, GDN_C), :]
    prev = ref[0, 0, hh, pl.ds(jnp.maximum(r0 - 8, 0), 8), :]
    prev = jnp.where(first, 0.0, prev)
    ext = jnp.concatenate([prev, cur], axis=0)
    acc = cur * cw[CONV_K - 1:CONV_K, :]
    for k in range(1, CONV_K):
        acc = acc + ext[8 - k:8 - k + GDN_C, :] * cw[CONV_K - 1 - k:CONV_K - k, :]
    return _silu(acc)


def _gdn_kernel(nq_ref, nk_ref, nv_ref, ng_ref, tail_ref, arow_ref, cwq_ref, cwk_ref, cwv_ref,
                alog_ref, dtb_ref, gn_ref, o_ref, gcr_ref):
    c = GDN_C
    n_chunks = nq_ref.shape[3] // c
    ri = lax.broadcasted_iota(jnp.int32, (c, c), 0)
    ci = lax.broadcasted_iota(jnp.int32, (c, c), 1)
    tril = ci <= ri
    strict = ci < ri
    tril_b = _ones_where(tril)
    triu_b = _ones_where(ri <= ci)
    gn = gn_ref[...]

    for hh in range(GDN_HEADS_PER_STEP):
        coef = -jnp.exp(alog_ref[hh])
        g_rows = coef[:, :1] * _softplus(arow_ref[hh, 0] + dtb_ref[hh][:, :1])
        g_rows = jnp.concatenate([g_rows, jnp.zeros((16 - n_chunks, c), F32)], axis=0)
        gcr_ref[hh] = _dot_exact_lhs(g_rows, triu_b)

    heads = range(GDN_HEADS_PER_STEP)

    def each(fn, *per_head):
        return [fn(*args) for args in zip(*per_head)]

    def chunk(n, states):
        r0 = pl.multiple_of(n * c, c)
        first = n == 0
        lanes = [slice(hh * HEAD_LANES, (hh + 1) * HEAD_LANES) for hh in heads]
        coef = [-jnp.exp(alog_ref[hh]) for hh in heads]
        dtb = [dtb_ref[hh] for hh in heads]
        sel_r = lax.broadcasted_iota(jnp.int32, (HEAD_LANES, 2 * HEAD_LANES), 0)
        sel_c = lax.broadcasted_iota(jnp.int32, (HEAD_LANES, 2 * HEAD_LANES), 1)
        hid = [pl.program_id(1) * GDN_HEADS_PER_STEP + hh for hh in heads]
        sel = [_ones_where(sel_r == jnp.where(sel_c < HEAD_LANES, h, DN_HEADS + h)) for h in hid]

        cq = [_causal_conv_silu(nq_ref, hh, r0, first, cwq_ref[:, lanes[hh]]) for hh in heads]
        ck = [_causal_conv_silu(nk_ref, hh, r0, first, cwk_ref[:, lanes[hh]]) for hh in heads]
        cv = [_causal_conv_silu(nv_ref, hh, r0, first, cwv_ref[:, lanes[hh]]) for hh in heads]
        qn = each(lambda a: (a * lax.rsqrt(jnp.sum(a * a, axis=-1, keepdims=True) + EPS))
                  * (DN_HEAD_DIM ** -0.5), cq)
        kn = each(lambda a: a * lax.rsqrt(jnp.sum(a * a, axis=-1, keepdims=True) + EPS), ck)

        tail_blk = tail_ref[pl.ds(r0, c), :]
        ba = each(lambda s_: _dot_exact_lhs(tail_blk, s_), sel)
        beta = each(lambda a: jax.nn.sigmoid(a[:, :HEAD_LANES]), ba)
        g_col = each(lambda a, cf, db: cf * _softplus(a[:, HEAD_LANES:] + db), ba, coef, dtb)
        gc = each(lambda g_: _dot_exact_rhs(tril_b, g_), g_col)
        gl = each(lambda g_: g_[c - 1:c, :], gc)
        eg = each(jnp.exp, gc)
        g_row = [gcr_ref[hh, pl.ds(n, 1), :] for hh in heads]

        kb = each(jnp.multiply, kn, beta)
        vb = each(jnp.multiply, cv, beta)
        kkqk = each(lambda kb_, qn_, kn_: _dot(jnp.concatenate([kb_, qn_], axis=0).astype(BF16),
                                               kn_.astype(BF16), NT), kb, qn, kn)
        decay = each(lambda g_, gr: jnp.exp(jnp.where(tril, jnp.concatenate([g_, g_], axis=1) - gr, -jnp.inf)),
                     gc, g_row)
        lmat = each(lambda kq, dc: jnp.where(strict, kq[:c] * dc, 0.0), kkqk, decay)
        attn = each(lambda kq, dc: kq[c:] * dc, kkqk, decay)

        same_base = (ri // GDN_BASE) == (ci // GDN_BASE)
        eye = jnp.where(ri == ci, 1.0, 0.0)
        p = each(lambda l_: jnp.where(same_base, l_, 0.0), lmat)
        tinv = each(lambda p_: eye - p_, p)
        for _ in range(int(math.log2(GDN_BASE)) - 1):
            p = each(lambda p_: _mm(p_, p_), p)
            tinv = each(lambda t_, p_: t_ + _mm(t_, p_), tinv, p)
        blk = GDN_BASE
        while blk < c:
            off_mask = (ri // (2 * blk) == ci // (2 * blk)) & (ri // blk != ci // blk)
            off = each(lambda l_: jnp.where(off_mask, l_, 0.0), lmat)
            half = each(_mm, tinv, off)
            tinv = each(lambda t_, h_: t_ - _mm(h_, t_), tinv, half)
            blk *= 2
        x = each(lambda t_, vb_, kb_, eg_: _mm(t_, jnp.concatenate([vb_, kb_ * eg_], axis=1)),
                 tinv, vb, kb, eg)

        sb = each(lambda s_: s_.astype(BF16), states)
        v_new = each(lambda x_, s_: x_[:, :HEAD_LANES] - _dot(x_[:, HEAD_LANES:].astype(BF16), s_), x, sb)
        vnb = each(lambda v_: v_.astype(BF16), v_new)
        out = each(lambda qn_, eg_, s_, at_, vn_: _dot((qn_ * eg_).astype(BF16), s_) + _dot(at_.astype(BF16), vn_),
                   qn, eg, sb, attn, vnb)
        k_tail = each(lambda kn_, gl_, g_: (kn_ * jnp.exp(gl_ - g_)).astype(BF16), kn, gl, gc)
        new_states = each(lambda st, gl_, kt, vn_: st * jnp.exp(gl_) + _dot(kt, vn_, ((0,), (0,))),
                          states, gl, k_tail, vnb)

        for hh in heads:
            o = out[hh]
            ms = jnp.mean(o * o, axis=-1, keepdims=True)
            y = (o * lax.rsqrt(ms + EPS)) * gn
            gate = ng_ref[0, 0, hh, pl.ds(r0, c), :]
            o_ref[0, pl.ds(r0, c), lanes[hh]] = (y * _silu(gate)).astype(o_ref.dtype)
        return tuple(new_states)

    state0 = tuple(jnp.zeros((DN_HEAD_DIM, DN_HEAD_DIM), F32) for _ in heads)
    lax.fori_loop(0, n_chunks, chunk, state0)


def _gated_deltanet(proj_main, tail, tail_t, conv_w, a_log_b, dt_bias_b, g_gnorm, batch, seq):
    c = GDN_C
    hps = GDN_HEADS_PER_STEP
    n_chunks = seq // c
    assert n_chunks <= 16 and DN_HEADS % hps == 0
    arow = tail_t.reshape(16, batch, n_chunks, c)
    head_spec = lambda g: pl.BlockSpec((1, 1, hps, seq, HEAD_LANES), lambda b, h: (g, b, h, 0, 0))
    cw_spec = lambda g: pl.BlockSpec((CONV_K, hps * HEAD_LANES), lambda b, h: (0, g * (DN_HEADS // hps) + h))
    vec_spec = pl.BlockSpec((hps, 1, HEAD_LANES), lambda b, h: (h, 0, 0))
    return pl.pallas_call(
        _gdn_kernel,
        grid=(batch, DN_HEADS // hps),
        in_specs=[
            head_spec(3), head_spec(4), head_spec(5), head_spec(6),
            pl.BlockSpec((seq, HEAD_LANES), lambda b, h: (b, 0)),
            pl.BlockSpec((hps, 1, n_chunks, c), lambda b, h: (DN_HEADS // hps + h, b, 0, 0)),
            cw_spec(0), cw_spec(1), cw_spec(2),
            vec_spec, vec_spec,
            pl.BlockSpec((1, DN_HEAD_DIM), lambda b, h: (0, 0)),
        ],
        out_specs=pl.BlockSpec((1, seq, hps * HEAD_LANES), lambda b, h: (b, 0, h)),
        out_shape=jax.ShapeDtypeStruct((batch, seq, DN_HEADS * DN_HEAD_DIM), BF16),
        scratch_shapes=[pltpu.VMEM((hps, 16, c), F32)],
        compiler_params=_cparams(("arbitrary", "arbitrary")),
        name="gated_deltanet",
    )(proj_main, proj_main, proj_main, proj_main, tail, arow, conv_w, conv_w, conv_w,
      a_log_b, dt_bias_b, g_gnorm)


OUT_TM = 512
OUT_TK = 512
EPI_ROWS = 256


def _pack_bf16_halves(x):
    n = x.shape[1] // 2
    bits = pltpu.bitcast(x.astype(BF16).astype(F32), jnp.int32)
    lo = lax.shift_right_logical(bits[:, :n], 16)
    hi = jnp.bitwise_and(bits[:, n:], jnp.int32(-65536))
    return jnp.bitwise_or(hi, lo)


def _unpack_bf16_halves(w):
    lo = pltpu.bitcast(lax.shift_left(w, 16), F32).astype(BF16)
    hi = pltpu.bitcast(jnp.bitwise_and(w, jnp.int32(-65536)), F32).astype(BF16)
    return lo, hi


def _dot_f32(a, b_parts):
    a_hi, a_mid, a_lo = _split3(a)
    b_hi, b_mid, b_lo = b_parts
    acc = _dot(a_hi, b_hi)
    acc += _dot(a_hi, b_mid) + _dot(a_mid, b_hi)
    acc += _dot(a_hi, b_lo) + _dot(a_mid, b_mid) + _dot(a_lo, b_hi)
    return acc


def _outproj_kernel(od_ref, on_ref, w_ref, x_ref, g1_ref, gf_ref, sc_ref, sh_ref, wr_ref, br_ref,
                    x1_ref, hn2_ref, idx_ref, gate_ref, rank_ref, cnt_ref, acc_ref, mh_ref, carry_ref):
    i = pl.program_id(0)
    k = pl.program_id(1)
    nk = pl.num_programs(1)
    tm = OUT_TM

    @pl.when((i == 0) & (k == 0))
    def _():
        carry_ref[...] = jnp.zeros_like(carry_ref)

    lhs = jnp.where(k < nk // 2, od_ref[...], on_ref[...])
    part = _dot(lhs, w_ref[...])

    @pl.when(k == 0)
    def _():
        acc_ref[...] = part

    @pl.when(k > 0)
    def _():
        acc_ref[...] += part

    @pl.when(k == nk - 1)
    def _():
        g1 = g1_ref[0]
        gf = gf_ref[...]
        sc = sc_ref[0]
        sh = sh_ref[0]
        wr_parts = _split3(wr_ref[...])
        br = br_ref[...]
        lane = lax.broadcasted_iota(jnp.int32, (EPI_ROWS, HEAD_LANES), 1)
        for r in range(0, tm, EPI_ROWS):
            rows = slice(r, r + EPI_ROWS)
            x1 = x_ref[rows, :] + g1 * acc_ref[rows, :]
            x1_ref[rows, :] = x1
            hn2 = _norm_modulate(x1, gf, sc, sh)
            hn2_ref[rows, :] = _pack_bf16_halves(hn2)
            logits = _dot_f32(hn2, wr_parts) + br
            l = jnp.where(lane < N_EXPERTS, logits, -jnp.inf)
            vals, idxs = [], []
            for _ in range(TOP_K):
                m = jnp.max(l, axis=-1, keepdims=True)
                ix = jnp.min(jnp.where(l == m, lane, HEAD_LANES), axis=-1, keepdims=True)
                vals.append(m)
                idxs.append(ix)
                l = jnp.where(lane == ix, -jnp.inf, l)
            es = [jnp.exp(v - vals[0]) for v in vals]
            den = (es[0] + es[1]) + (es[2] + es[3])
            idx_o = jnp.zeros((EPI_ROWS, HEAD_LANES), jnp.int32)
            gate_o = jnp.zeros((EPI_ROWS, HEAD_LANES), F32)
            hot = jnp.zeros((EPI_ROWS, HEAD_LANES), F32)
            for kk in range(TOP_K):
                idx_o = jnp.where(lane == kk, idxs[kk], idx_o)
                gate_o = jnp.where(lane == kk, es[kk] / den, gate_o)
                hot = jnp.where(lane == idxs[kk], 1.0, hot)
            idx_ref[rows, :] = idx_o
            gate_ref[rows, :] = gate_o
            mh_ref[rows, :] = hot.astype(BF16)

        ri = lax.broadcasted_iota(jnp.int32, (tm, tm), 0)
        ci = lax.broadcasted_iota(jnp.int32, (tm, tm), 1)
        mh = mh_ref[...]
        carry = carry_ref[0:1, :]
        cum = _dot(_ones_where(ci < ri), mh) + carry
        lane_t = lax.broadcasted_iota(jnp.int32, (tm, HEAD_LANES), 1)
        idx_t = idx_ref[...]
        rank_o = jnp.zeros((tm, HEAD_LANES), F32)
        for kk in range(TOP_K):
            ik = idx_t[:, kk:kk + 1]
            rk = jnp.sum(jnp.where(lane_t == ik, cum, 0.0), axis=-1, keepdims=True)
            rank_o = jnp.where(lane_t == kk, rk, rank_o)
        rank_ref[...] = rank_o.astype(jnp.int32)
        total = carry + jnp.sum(mh.astype(F32), axis=0, keepdims=True)
        carry_ref[...] = jnp.broadcast_to(total, carry_ref.shape)
        cnt_ref[...] = jnp.broadcast_to(total, cnt_ref.shape).astype(jnp.int32)


def _outproj_router(o_diff, o_dn, w_out, x2, gate1, g_ffn, scale2, shift2, w_router_pad, b_router_pad, seq):
    n_tok, d = x2.shape
    tm, tk = OUT_TM, OUT_TK
    tiles_per_batch = seq // tm
    nk = d // tk
    half = nk // 2
    vec3 = pl.BlockSpec((1, 1, d), lambda i, k: (i // tiles_per_batch, 0, 0))
    row_out = lambda dt: jax.ShapeDtypeStruct((n_tok, HEAD_LANES), dt)
    return pl.pallas_call(
        _outproj_kernel,
        grid=(n_tok // tm, nk),
        in_specs=[
            pl.BlockSpec((tm, tk), lambda i, k: (i, jnp.minimum(k, half - 1))),
            pl.BlockSpec((tm, tk), lambda i, k: (i, jnp.maximum(k - half, 0))),
            pl.BlockSpec((tk, d), lambda i, k: (k, 0)),
            pl.BlockSpec((tm, d), lambda i, k: (i, 0)),
            vec3,
            pl.BlockSpec((1, d), lambda i, k: (0, 0)),
            vec3, vec3,
            pl.BlockSpec((d, HEAD_LANES), lambda i, k: (0, 0)),
            pl.BlockSpec((1, HEAD_LANES), lambda i, k: (0, 0)),
        ],
        out_specs=[
            pl.BlockSpec((tm, d), lambda i, k: (i, 0)),
            pl.BlockSpec((tm, d // 2), lambda i, k: (i, 0)),
            pl.BlockSpec((tm, HEAD_LANES), lambda i, k: (i, 0)),
            pl.BlockSpec((tm, HEAD_LANES), lambda i, k: (i, 0)),
            pl.BlockSpec((tm, HEAD_LANES), lambda i, k: (i, 0)),
            pl.BlockSpec((8, HEAD_LANES), lambda i, k: (0, 0)),
        ],
        out_shape=[
            jax.ShapeDtypeStruct((n_tok, d), F32),
            jax.ShapeDtypeStruct((n_tok, d // 2), jnp.int32),
            row_out(jnp.int32), row_out(F32), row_out(jnp.int32),
            jax.ShapeDtypeStruct((8, HEAD_LANES), jnp.int32),
        ],
        scratch_shapes=[pltpu.VMEM((tm, d), F32), pltpu.VMEM((tm, HEAD_LANES), BF16),
                        pltpu.VMEM((8, HEAD_LANES), F32)],
        compiler_params=_cparams(("arbitrary", "arbitrary")),
        name="outproj_router",
    )(o_diff, o_dn, w_out, x2, gate1, g_ffn, scale2, shift2, w_router_pad, b_router_pad)


MOE_TM = 2048
MOE_SUB = 256
MOE_CHUNK = 512
MOE_TF = 256
MOE_TN = 512
DSP_TM = 256


def _row_copy(src_hbm, row, dst_vmem, slot, sem):
    return pltpu.make_async_copy(src_hbm.at[pl.ds(row, 1)], dst_vmem.at[pl.ds(slot, 1)], sem)


def _dispatch_kernel(sbr_ref, sbm_ref, dest_ref, hn2_ref, xs_hbm, zero_ref, sem):
    i = pl.program_id(0)
    tm = DSP_TM

    @pl.when(i == 0)
    def _():
        zero_ref[...] = jnp.zeros_like(zero_ref)
        n_sb = sbr_ref.shape[0]

        def pad_copies(s, fn):
            nv = sbr_ref[s]
            nv8 = (nv + 7) // 8 * 8
            for t in range(7):
                @pl.when(nv + t < nv8)
                def _():
                    fn(pltpu.make_async_copy(zero_ref.at[pl.ds(0, 1)],
                                             xs_hbm.at[pl.ds(s * MOE_TM + nv + t, 1)], sem))

            pad8 = (sbm_ref[s] - nv8) // 8
            bit = MOE_SUB // 16
            while bit >= 1:
                rows = 8 * bit
                start = pl.multiple_of(s * MOE_TM + nv8 + 8 * (pad8 & ~(2 * bit - 1)), 8)

                @pl.when((pad8 & bit) != 0)
                def _():
                    fn(pltpu.make_async_copy(zero_ref.at[pl.ds(0, rows)],
                                             xs_hbm.at[pl.ds(start, rows)], sem))

                bit //= 2
            return 0

        lax.fori_loop(0, n_sb, lambda s, c: c + pad_copies(s, lambda cp: cp.start()), 0)
        lax.fori_loop(0, n_sb, lambda s, c: c + pad_copies(s, lambda cp: cp.wait()), 0)

    def issue(r, carry):
        for kk in range(TOP_K):
            pltpu.make_async_copy(hn2_ref.at[pl.ds(r, 1)],
                                  xs_hbm.at[pl.ds(dest_ref[0, 0, kk * tm + r], 1)], sem).start(priority=kk % 2)
        return carry

    lax.fori_loop(0, tm, issue, 0)

    def wait(r, carry):
        for kk in range(TOP_K):
            pltpu.make_async_copy(hn2_ref.at[pl.ds(0, 1)], xs_hbm.at[pl.ds(0, 1)], sem).wait()
        return carry

    lax.fori_loop(0, tm, wait, 0)


def _dispatch(sb_rows, sb_m, dest_tiles, hn2, n_rows):
    n_tok, d = hn2.shape
    tm = DSP_TM
    grid_spec = pltpu.PrefetchScalarGridSpec(
        num_scalar_prefetch=2,
        grid=(n_tok // tm,),
        in_specs=[
            pl.BlockSpec((1, 1, tm * TOP_K), lambda i, sbr, sbm: (i, 0, 0), memory_space=pltpu.SMEM),
            pl.BlockSpec((tm, d), lambda i, sbr, sbm: (i, 0)),
        ],
        out_specs=pl.BlockSpec(memory_space=pl.ANY),
        scratch_shapes=[pltpu.VMEM((MOE_SUB // 2, d), hn2.dtype), pltpu.SemaphoreType.DMA(())],
    )
    return pl.pallas_call(
        _dispatch_kernel,
        grid_spec=grid_spec,
        out_shape=jax.ShapeDtypeStruct((n_rows, d), hn2.dtype),
        compiler_params=_cparams(("arbitrary",)),
        name="moe_dispatch",
    )(sb_rows, sb_m, dest_tiles, hn2)


def _for_row_chunks(m, body):
    n_full = m // MOE_CHUNK

    def step(c, carry):
        body(pl.multiple_of(c * MOE_CHUNK, MOE_CHUNK), MOE_CHUNK)
        return carry

    lax.fori_loop(0, n_full, step, 0)
    assert MOE_CHUNK == 2 * MOE_SUB

    @pl.when(m - n_full * MOE_CHUNK > 0)
    def _():
        body(pl.multiple_of(n_full * MOE_CHUNK, MOE_CHUNK), MOE_SUB)


def _for_static_rows(m, body):
    n_full = m // MOE_CHUNK
    for n in range(1, MOE_TM // MOE_CHUNK + 1):
        @pl.when(n_full == n)
        def _():
            body(0, n * MOE_CHUNK)

    assert MOE_CHUNK == 2 * MOE_SUB

    @pl.when(m - n_full * MOE_CHUNK > 0)
    def _():
        body(pl.multiple_of(n_full * MOE_CHUNK, MOE_CHUNK), MOE_SUB)


def _moe_up_kernel(sbe_ref, sbm_ref, nused_ref, x_ref, wg_ref, wu_ref, bg_ref, bu_ref, h_ref, xb_ref):
    s = pl.program_id(0)
    f = pl.program_id(1)
    m = sbm_ref[s]
    half = x_ref.shape[1]

    @pl.when(f == 0)
    def _():
        def unpack(r0, rows):
            lo, hi = _unpack_bf16_halves(x_ref[pl.ds(r0, rows), :])
            xb_ref[pl.ds(r0, rows), 0:half] = lo
            xb_ref[pl.ds(r0, rows), half:2 * half] = hi

        _for_row_chunks(m, unpack)

    def compute(r0, rows):
        xq = xb_ref[pl.ds(r0, rows), :]
        gate = jnp.minimum(_dot(xq, wg_ref[0].astype(BF16)) + bg_ref[0], SWIGLU_LIMIT)
        up = jnp.clip(_dot(xq, wu_ref[0].astype(BF16)) + bu_ref[0], -SWIGLU_LIMIT, SWIGLU_LIMIT)
        glu = gate * jax.nn.sigmoid(gate * SWIGLU_ALPHA)
        h_ref[pl.ds(r0, rows), :] = ((up + 1.0) * glu).astype(BF16)

    _for_static_rows(m, compute)


def _moe_up(sb_expert, sb_m, n_used, xs, w_gate_up, b_gate_up):
    n_sb = sb_expert.shape[0]
    d = xs.shape[1]
    n_exp, dm, f2 = w_gate_up.shape
    assert dm == 2 * d
    f_dim = f2 // 2
    nf = f_dim // MOE_TF

    def f_eff(s, f, nu):
        return jnp.where(s < nu[0], f, nf - 1)

    def s_eff(s, nu):
        return jnp.minimum(s, nu[0] - 1)

    grid_spec = pltpu.PrefetchScalarGridSpec(
        num_scalar_prefetch=3,
        grid=(n_sb, nf),
        in_specs=[
            pl.BlockSpec((MOE_TM, d), lambda s, f, sbe, sbm, nu: (s_eff(s, nu), 0)),
            pl.BlockSpec((1, dm, MOE_TF), lambda s, f, sbe, sbm, nu: (sbe[s], 0, f_eff(s, f, nu))),
            pl.BlockSpec((1, dm, MOE_TF), lambda s, f, sbe, sbm, nu: (sbe[s], 0, nf + f_eff(s, f, nu))),
            pl.BlockSpec((1, 1, MOE_TF), lambda s, f, sbe, sbm, nu: (sbe[s], 0, f_eff(s, f, nu))),
            pl.BlockSpec((1, 1, MOE_TF), lambda s, f, sbe, sbm, nu: (sbe[s], 0, nf + f_eff(s, f, nu))),
        ],
        out_specs=pl.BlockSpec((MOE_TM, MOE_TF), lambda s, f, sbe, sbm, nu: (s_eff(s, nu), f_eff(s, f, nu))),
        scratch_shapes=[pltpu.VMEM((MOE_TM, 2 * d), BF16)],
    )
    b3 = b_gate_up.reshape(n_exp, 1, f2)
    return pl.pallas_call(
        _moe_up_kernel,
        grid_spec=grid_spec,
        out_shape=jax.ShapeDtypeStruct((n_sb * MOE_TM, f_dim), BF16),
        compiler_params=_cparams(("arbitrary", "arbitrary")),
        name="moe_up",
    )(sb_expert, sb_m, n_used, xs, w_gate_up, w_gate_up, b3, b3)


def _moe_down_kernel(sbe_ref, sbm_ref, nused_ref, h_ref, wd_ref, bd_ref, o_ref):
    s = pl.program_id(0)
    m = sbm_ref[s]

    def compute(r0, rows):
        o_ref[pl.ds(r0, rows), :] = _dot(h_ref[pl.ds(r0, rows), :], wd_ref[0].astype(BF16)) + bd_ref[0]

    _for_static_rows(m, compute)


def _moe_down(sb_expert, sb_m, n_used, h, w_down, b_down):
    n_sb = sb_expert.shape[0]
    n_exp, f_dim, d = w_down.shape
    nn = d // MOE_TN

    def n_eff(s, n, nu):
        return jnp.where(s < nu[0], n, nn - 1)

    def s_eff(s, nu):
        return jnp.minimum(s, nu[0] - 1)

    grid_spec = pltpu.PrefetchScalarGridSpec(
        num_scalar_prefetch=3,
        grid=(n_sb, nn),
        in_specs=[
            pl.BlockSpec((MOE_TM, f_dim), lambda s, n, sbe, sbm, nu: (s_eff(s, nu), 0)),
            pl.BlockSpec((1, f_dim, MOE_TN), lambda s, n, sbe, sbm, nu: (sbe[s], 0, n_eff(s, n, nu))),
            pl.BlockSpec((1, 1, MOE_TN), lambda s, n, sbe, sbm, nu: (sbe[s], 0, n_eff(s, n, nu))),
        ],
        out_specs=pl.BlockSpec((MOE_TM, MOE_TN), lambda s, n, sbe, sbm, nu: (s_eff(s, nu), n_eff(s, n, nu))),
    )
    return pl.pallas_call(
        _moe_down_kernel,
        grid_spec=grid_spec,
        out_shape=jax.ShapeDtypeStruct((n_sb * MOE_TM, d), F32),
        compiler_params=_cparams(("arbitrary", "arbitrary")),
        name="moe_down",
    )(sb_expert, sb_m, n_used, h, w_down, b_down.reshape(n_exp, 1, d))


CMB_TM = 256


def _combine_kernel(dest_ref, ys_hbm, x1_ref, gate_ref, g2_ref, gfin_ref, o_ref, buf_ref, sem):
    tm = CMB_TM

    def issue(r, carry):
        for kk in range(TOP_K):
            _row_copy(ys_hbm, dest_ref[0, 0, kk * tm + r], buf_ref.at[kk], r, sem).start(priority=kk % 2)
        return carry

    lax.fori_loop(0, tm, issue, 0)

    def wait(r, carry):
        for kk in range(TOP_K):
            _row_copy(ys_hbm, 0, buf_ref.at[kk], 0, sem).wait()
        return carry

    lax.fori_loop(0, tm, wait, 0)

    gates = gate_ref[...]
    y = gates[:, 0:1] * buf_ref[0]
    for kk in range(1, TOP_K):
        y = y + gates[:, kk:kk + 1] * buf_ref[kk]
    x2 = x1_ref[...] + g2_ref[0] * y
    ms = jnp.mean(x2 * x2, axis=-1, keepdims=True)
    o_ref[...] = (x2 * lax.rsqrt(ms + EPS)) * gfin_ref[...]


def _combine(dest_tiles, ys, x1, gates, gate2, g_final, seq):
    n_tok, d = x1.shape
    tm = CMB_TM
    tiles_per_batch = seq // tm
    return pl.pallas_call(
        _combine_kernel,
        grid=(n_tok // tm,),
        in_specs=[
            pl.BlockSpec((1, 1, tm * TOP_K), lambda i: (i, 0, 0), memory_space=pltpu.SMEM),
            pl.BlockSpec(memory_space=pl.ANY),
            pl.BlockSpec((tm, d), lambda i: (i, 0)),
            pl.BlockSpec((tm, HEAD_LANES), lambda i: (i, 0)),
            pl.BlockSpec((1, 1, d), lambda i: (i // tiles_per_batch, 0, 0)),
            pl.BlockSpec((1, d), lambda i: (0, 0)),
        ],
        out_specs=pl.BlockSpec((tm, d), lambda i: (i, 0)),
        out_shape=jax.ShapeDtypeStruct((n_tok, d), F32),
        scratch_shapes=[pltpu.VMEM((TOP_K, tm, d), F32), pltpu.SemaphoreType.DMA(())],
        compiler_params=_cparams(("arbitrary",)),
        name="moe_combine_norm",
    )(dest_tiles, ys, x1, gates, gate2, g_final)


def _route_tables(top_idx, rank, counts, n_tok):
    n_sb = n_tok * TOP_K // MOE_TM + N_EXPERTS
    max_sb_per_e = n_tok // MOE_TM
    cnt = counts[0, :N_EXPERTS]
    sb_per_e = (cnt + MOE_TM - 1) // MOE_TM
    rows_per_sb = (cnt + jnp.maximum(sb_per_e, 1) - 1) // jnp.maximum(sb_per_e, 1)
    sb_end = jnp.cumsum(sb_per_e)
    sb_start = sb_end - sb_per_e
    n_used = sb_end[-1]
    s_ids = jnp.arange(n_sb, dtype=jnp.int32)
    e_ids = jnp.arange(N_EXPERTS, dtype=jnp.int32)
    s_clamped = jnp.minimum(s_ids, n_used - 1)
    e_of_s = jnp.sum((sb_end[None, :] <= s_clamped[:, None]).astype(jnp.int32), axis=1)
    e_of_s = jnp.minimum(e_of_s, N_EXPERTS - 1).astype(jnp.int32)
    onehot_s = (e_of_s[:, None] == e_ids[None, :]).astype(jnp.int32)
    j_of_s = s_ids - jnp.sum(onehot_s * sb_start[None, :], axis=1)
    per_s = jnp.sum(onehot_s * rows_per_sb[None, :], axis=1)
    cnt_s = jnp.sum(onehot_s * cnt[None, :], axis=1)
    sb_rows = jnp.where(s_ids < n_used, jnp.clip(cnt_s - j_of_s * per_s, 0, per_s), 0).astype(jnp.int32)
    sb_m = ((sb_rows + MOE_SUB - 1) // MOE_SUB * MOE_SUB).astype(jnp.int32)
    idx_t = top_idx[:, :TOP_K].T
    rank_t = rank[:, :TOP_K].T
    onehot_a = idx_t[None, :, :] == e_ids[:, None, None]
    start_a = jnp.sum(jnp.where(onehot_a, sb_start[:, None, None], 0), axis=0)
    per_a = jnp.sum(jnp.where(onehot_a, rows_per_sb[:, None, None], 0), axis=0)
    sb_in_e = sum((rank_t >= j * per_a).astype(jnp.int32) for j in range(1, max_sb_per_e))
    dest_t = ((start_a + sb_in_e) * MOE_TM + (rank_t - sb_in_e * per_a)).astype(jnp.int32)
    return e_of_s, sb_rows, sb_m, n_used.reshape(1).astype(jnp.int32), dest_t, n_sb


def _moe_block(hn2_packed, top_idx, rank, counts, x1, gates, gate2, w_gate_up, b_gate_up, w_down, b_down,
               g_final, seq):
    n_tok = x1.shape[0]
    e_of_s, sb_rows, sb_m, n_used1, dest_t, n_sb = _route_tables(top_idx, rank, counts, n_tok)

    def dest_tiles(tm):
        return dest_t.reshape(TOP_K, n_tok // tm, tm).transpose(1, 0, 2).reshape(n_tok // tm, 1, TOP_K * tm)

    xs = _dispatch(sb_rows, sb_m, dest_tiles(DSP_TM), hn2_packed, n_sb * MOE_TM)
    hmid = _moe_up(e_of_s, sb_m, n_used1, xs, w_gate_up, b_gate_up)
    ys = _moe_down(e_of_s, sb_m, n_used1, hmid, w_down, b_down)
    return _combine(dest_tiles(CMB_TM), ys, x1, gates, gate2, g_final[None, :], seq)


def kernel(x, c, w_ada, b_ada, g_norm_mix, g_norm_ffn, g_norm_final, w_in, conv_w, rel_bias,
           lambda_q1, lambda_k1, lambda_q2, lambda_k2, g_subln, a_log, dt_bias, g_gnorm, w_out,
           w_router, b_router, w_gate_up, b_gate_up, w_down, b_down):
    batch, seq, d = x.shape
    n_tok = batch * seq
    depth = w_ada.shape[0]
    assert depth == 1 and d == D_MODEL
    x2 = x.reshape(n_tok, d)
    bias_tiles = _bias_tiles(rel_bias)
    for l in range(depth):
        c_pad = jnp.zeros((16, d), F32).at[:batch].set(c)
        mod = _adaln(c_pad, w_ada[l], b_ada[l][None, :])[:batch]
        shift1, scale1, gate1, shift2, scale2, gate2 = [m.reshape(batch, 1, d) for m in jnp.split(mod, 6, axis=-1)]

        w_tail = jnp.pad(w_in[l, :, MAIN_COLS:], ((0, 0), (0, HEAD_LANES - 2 * DN_HEADS)))
        w_tail_t = w_in[l, :, MAIN_COLS:].T
        proj_main, tail, tail_t = _inproj(x2, g_norm_mix[l][None, :], scale1, shift1,
                                          w_in[l, :, :MAIN_COLS].astype(BF16), w_tail, w_tail_t, batch, seq)
        o_diff = _diff_attention(proj_main, bias_tiles, lambda_q1[l][None, :], lambda_k1[l][None, :],
                                 lambda_q2[l][None, :], lambda_k2[l][None, :], g_subln[l][None, :], batch, seq)
        a_log_b = jnp.broadcast_to(a_log[l][:, None, None], (DN_HEADS, 1, HEAD_LANES))
        dt_bias_b = jnp.broadcast_to(dt_bias[l][:, None, None], (DN_HEADS, 1, HEAD_LANES))
        o_dn = _gated_deltanet(proj_main, tail, tail_t, conv_w[l], a_log_b, dt_bias_b, g_gnorm[l][None, :],
                               batch, seq)

        w_router_pad = jnp.pad(w_router[l], ((0, 0), (0, HEAD_LANES - N_EXPERTS)))
        b_router_pad = jnp.pad(b_router[l], (0, HEAD_LANES - N_EXPERTS))[None, :]
        x1, hn2, top_idx, gates, rank, counts = _outproj_router(
            o_diff.reshape(n_tok, -1), o_dn.reshape(n_tok, -1), w_out[l].astype(BF16), x2, gate1, g_norm_ffn[l][None, :],
            scale2, shift2, w_router_pad, b_router_pad, seq)

        x2 = _moe_block(hn2, top_idx, rank, counts, x1, gates, gate2, w_gate_up[l], b_gate_up[l], w_down[l],
                        b_down[l], g_norm_final, seq)
    return x2.reshape(batch, seq, d)
```

```python
import functools
import math

import jax
import jax.numpy as jnp
import numpy as np
from jax import lax
from jax.experimental import pallas as pl
from jax.experimental.pallas import tpu as pltpu

F32 = jnp.float32
BF16 = jnp.bfloat16

D_MODEL = 2048
DIFF_HEADS = 8
DIFF_HEAD_DIM = 64
DIFF_V_DIM = 128
DN_HEADS = 8
DN_HEAD_DIM = 128
HEAD_LANES = 128
CONV_K = 4
N_BUCKETS = 32
MAX_DISTANCE = 128
N_EXPERTS = 32
TOP_K = 4
D_EXPERT = 2048
SWIGLU_LIMIT = 7.0
SWIGLU_ALPHA = 1.702
EPS = 1e-6
LAMBDA_INIT = 0.8 - 0.6 * math.exp(-0.3 * 0)
N_MAIN_GROUPS = 7
MAIN_COLS = N_MAIN_GROUPS * 1024

V7X_VMEM_BYTES = 64 * 1024 * 1024
VMEM_LIMIT = 56 * 1024 * 1024

NEG_BIG = -1e30


def _cparams(sem):
    return pltpu.CompilerParams(dimension_semantics=sem, vmem_limit_bytes=VMEM_LIMIT)


def _split3(a):
    hi = a.astype(BF16)
    r1 = a - hi.astype(F32)
    mid = r1.astype(BF16)
    lo = (r1 - mid.astype(F32)).astype(BF16)
    return hi, mid, lo


def _dot(a, b, dims=None):
    if dims is None:
        return jnp.dot(a, b, preferred_element_type=F32)
    return lax.dot_general(a, b, (dims, ((), ())), preferred_element_type=F32)


def _dot_exact_lhs(a, b_bf16, dims=None):
    hi, mid, lo = _split3(a)
    return (_dot(hi, b_bf16, dims) + _dot(mid, b_bf16, dims)) + _dot(lo, b_bf16, dims)


def _dot_exact_rhs(a_bf16, b, dims=None):
    hi, mid, lo = _split3(b)
    return (_dot(a_bf16, hi, dims) + _dot(a_bf16, mid, dims)) + _dot(a_bf16, lo, dims)


NT = ((1,), (1,))


def _ones_where(mask):
    return jnp.where(mask, 1.0, 0.0).astype(BF16)


ADA_TN = 1024


def _adaln_kernel(c_ref, w_ref, b_ref, o_ref):
    c = c_ref[...]
    sc = c * jax.nn.sigmoid(c)
    a_hi, a_mid, a_lo = _split3(sc)
    w_hi, w_mid, w_lo = _split3(w_ref[...])
    acc = _dot(a_hi, w_hi)
    acc += _dot(a_hi, w_mid) + _dot(a_mid, w_hi)
    acc += _dot(a_hi, w_lo) + _dot(a_mid, w_mid) + _dot(a_lo, w_hi)
    o_ref[...] = acc + b_ref[...]


def _adaln(c_pad, w, b):
    n = w.shape[1]
    return pl.pallas_call(
        _adaln_kernel,
        grid=(n // ADA_TN,),
        in_specs=[
            pl.BlockSpec(c_pad.shape, lambda j: (0, 0)),
            pl.BlockSpec((w.shape[0], ADA_TN), lambda j: (0, j)),
            pl.BlockSpec((1, ADA_TN), lambda j: (0, j)),
        ],
        out_specs=pl.BlockSpec((c_pad.shape[0], ADA_TN), lambda j: (0, j)),
        out_shape=jax.ShapeDtypeStruct((c_pad.shape[0], n), F32),
        compiler_params=_cparams(("arbitrary",)),
        name="adaln_mod",
    )(c_pad, w, b)


INPROJ_TM = 1024
INPROJ_TN = 1024
NORM_ROWS = 256


def _norm_modulate(x, g, scale, shift):
    ms = jnp.mean(x * x, axis=-1, keepdims=True)
    y = (x * lax.rsqrt(ms + EPS)) * g
    return y * (1.0 + scale) + shift


def _inproj_kernel(x_ref, g_ref, sc_ref, sh_ref, w_ref, wt_ref, wtt_ref, o_ref, t_ref, tt_ref, hn_ref):
    j = pl.program_id(1)

    @pl.when(j == 0)
    def _():
        g = g_ref[...]
        sc = sc_ref[0]
        sh = sh_ref[0]
        for r in range(0, INPROJ_TM, NORM_ROWS):
            hn = _norm_modulate(x_ref[r:r + NORM_ROWS, :], g, sc, sh)
            hn_ref[r:r + NORM_ROWS, :] = hn.astype(BF16)
        hnb = hn_ref[...]
        t_ref[...] = _dot(hnb, wt_ref[...].astype(BF16))
        tt_ref[...] = _dot(wtt_ref[...].astype(BF16), hnb, NT)

    acc = _dot(hn_ref[...], w_ref[...])
    for hh in range(INPROJ_TN // HEAD_LANES):
        o_ref[0, 0, hh] = acc[:, hh * HEAD_LANES:(hh + 1) * HEAD_LANES]


def _inproj(x2, g_mix, scale1, shift1, w_in, w_tail, w_tail_t, batch, seq):
    n_tok, d = x2.shape
    tiles_per_batch = seq // INPROJ_TM
    heads_per_tile = INPROJ_TN // HEAD_LANES
    tiles_per_group = 1024 // INPROJ_TN
    grid = (n_tok // INPROJ_TM, MAIN_COLS // INPROJ_TN)
    return pl.pallas_call(
        _inproj_kernel,
        grid=grid,
        in_specs=[
            pl.BlockSpec((INPROJ_TM, d), lambda i, j: (i, 0)),
            pl.BlockSpec((1, d), lambda i, j: (0, 0)),
            pl.BlockSpec((1, 1, d), lambda i, j: (i // tiles_per_batch, 0, 0)),
            pl.BlockSpec((1, 1, d), lambda i, j: (i // tiles_per_batch, 0, 0)),
            pl.BlockSpec((d, INPROJ_TN), lambda i, j: (0, j)),
            pl.BlockSpec((d, HEAD_LANES), lambda i, j: (0, 0)),
            pl.BlockSpec((16, d), lambda i, j: (0, 0)),
        ],
        out_specs=[
            pl.BlockSpec((1, 1, heads_per_tile, INPROJ_TM, HEAD_LANES),
                         lambda i, j: (j // tiles_per_group, i // tiles_per_batch, j % tiles_per_group,
                                       i % tiles_per_batch, 0)),
            pl.BlockSpec((INPROJ_TM, HEAD_LANES), lambda i, j: (i, 0)),
            pl.BlockSpec((16, INPROJ_TM), lambda i, j: (0, i)),
        ],
        out_shape=[
            jax.ShapeDtypeStruct((N_MAIN_GROUPS, batch, 8, seq, HEAD_LANES), F32),
            jax.ShapeDtypeStruct((n_tok, HEAD_LANES), F32),
            jax.ShapeDtypeStruct((16, n_tok), F32),
        ],
        scratch_shapes=[pltpu.VMEM((INPROJ_TM, d), BF16)],
        compiler_params=_cparams(("arbitrary", "arbitrary")),
        name="norm_inproj",
    )(x2, g_mix, scale1, shift1, w_in, w_tail, w_tail_t)


ATT_T = 256
ATT_ROWS = 128


def _t5_bucket(n):
    n = jnp.maximum(n, 0)
    max_exact = N_BUCKETS // 2
    nf = jnp.maximum(n, max_exact).astype(F32)
    large = max_exact + (jnp.log(nf / max_exact) / math.log(MAX_DISTANCE / max_exact)
                         * (N_BUCKETS - max_exact)).astype(jnp.int32)
    large = jnp.minimum(large, N_BUCKETS - 1)
    return jnp.where(n < max_exact, n, large)


def _bias_tiles(rel_bias):
    assert ATT_T >= MAX_DISTANCE
    t = ATT_T
    table = rel_bias.astype(F32)
    n_heads = table.shape[1]
    tiles = []
    for d in range(2):
        dist = (d * t + t) - jnp.arange(2 * t, dtype=jnp.int32)
        w = jnp.where(dist[:, None] >= 0, table[_t5_bucket(dist)], F32(NEG_BIG)).T
        w_ext = jnp.concatenate([w, jnp.zeros((n_heads, 1), F32)], axis=1)
        skew = jnp.tile(w_ext, (1, t))[:, :t * 2 * t].reshape(n_heads, t, 2 * t)
        tiles.append(skew[:, :, t:])
    far = jnp.broadcast_to(table[N_BUCKETS - 1][:, None, None], (n_heads, t, t))
    return jnp.stack(tiles + [far], axis=1)


def _attn_kernel(q_ref, k_ref, v_ref, bias_ref, lq1_ref, lk1_ref, lq2_ref, lk2_ref, gs_ref,
                 o_ref, kb_ref, vb_ref, s0_ref, s1_ref, p_ref, m_ref, l_ref, acc_ref):
    i = pl.program_id(2)
    t = ATT_T
    rows2 = 2 * t

    @pl.when(i == 0)
    def _():
        kb_ref[...] = k_ref[0, 0, 0].astype(BF16)
        vb_ref[...] = v_ref[0, 0, 0].astype(BF16)

    q = q_ref[0, 0, 0] * (DIFF_HEAD_DIM ** -0.5)
    lane = lax.broadcasted_iota(jnp.int32, q.shape, 1)
    q1 = jnp.where(lane < DIFF_HEAD_DIM, q, 0.0)
    q2 = jnp.where(lane >= DIFF_HEAD_DIM, q, 0.0)
    q12 = jnp.concatenate([q1, q2], axis=0).astype(BF16)

    m_ref[...] = jnp.full(m_ref.shape, -jnp.inf, F32)
    l_ref[...] = jnp.zeros(l_ref.shape, F32)
    acc_ref[...] = jnp.zeros(acc_ref.shape, F32)

    def scores(j):
        return _dot(q12, kb_ref[pl.ds(pl.multiple_of(j * t, t), t), :], NT)

    def block(j, s_cur_ref, s_next_ref):
        s_next_ref[...] = scores(jnp.minimum(j + 1, i))
        d = jnp.minimum(i - j, 2)
        for r in range(0, rows2, ATT_ROWS):
            rows = slice(r, r + ATT_ROWS)
            brows = slice(r % t, r % t + ATT_ROWS)
            halves = [s_cur_ref[rows, c:c + HEAD_LANES] + bias_ref[0, d, brows, c:c + HEAD_LANES]
                      for c in range(0, t, HEAD_LANES)]
            m_old = m_ref[rows, :]
            m_new = jnp.maximum(m_old, jnp.max(functools.reduce(jnp.maximum, halves), axis=-1, keepdims=True))
            alpha = jnp.exp(m_old - m_new)
            ps = [jnp.exp(sh - m_new) for sh in halves]
            l_ref[rows, :] = alpha * l_ref[rows, :] + jnp.sum(functools.reduce(jnp.add, ps), axis=-1, keepdims=True)
            m_ref[rows, :] = m_new
            for c, ph in zip(range(0, t, HEAD_LANES), ps):
                p_ref[rows, c:c + HEAD_LANES] = ph.astype(BF16)
            acc_ref[rows, :] = alpha * acc_ref[rows, :]
        acc_ref[...] += _dot(p_ref[...], vb_ref[pl.ds(pl.multiple_of(j * t, t), t), :])

    s0_ref[...] = scores(0)

    def pair(pidx, carry):
        j0 = 2 * pidx
        block(j0, s0_ref, s1_ref)

        @pl.when(j0 + 1 <= i)
        def _():
            block(j0 + 1, s1_ref, s0_ref)

        return carry

    lax.fori_loop(0, (i + 2) // 2, pair, 0)

    lam = (jnp.exp(jnp.sum(lq1_ref[...] * lk1_ref[...], axis=-1, keepdims=True))
           - jnp.exp(jnp.sum(lq2_ref[...] * lk2_ref[...], axis=-1, keepdims=True)) + LAMBDA_INIT)
    o = acc_ref[0:t, :] / l_ref[0:t, :] - lam * (acc_ref[t:rows2, :] / l_ref[t:rows2, :])
    ms = jnp.mean(o * o, axis=-1, keepdims=True)
    y = (o * lax.rsqrt(ms + EPS)) * gs_ref[...]
    o_ref[0] = (y * (1.0 - LAMBDA_INIT)).astype(o_ref.dtype)


def _diff_attention(proj_main, bias_tiles, lq1, lk1, lq2, lk2, g_subln, batch, seq):
    t = ATT_T
    nq = seq // t
    lam_spec = pl.BlockSpec((1, DIFF_HEAD_DIM), lambda b, h, i: (0, 0))
    return pl.pallas_call(
        _attn_kernel,
        grid=(batch, DIFF_HEADS, nq),
        in_specs=[
            pl.BlockSpec((1, 1, 1, t, HEAD_LANES), lambda b, h, i: (0, b, h, i, 0)),
            pl.BlockSpec((1, 1, 1, seq, HEAD_LANES), lambda b, h, i: (1, b, h, 0, 0)),
            pl.BlockSpec((1, 1, 1, seq, HEAD_LANES), lambda b, h, i: (2, b, h, 0, 0)),
            pl.BlockSpec((1, 3, t, t), lambda b, h, i: (h, 0, 0, 0)),
            lam_spec, lam_spec, lam_spec, lam_spec,
            pl.BlockSpec((1, DIFF_V_DIM), lambda b, h, i: (0, 0)),
        ],
        out_specs=pl.BlockSpec((1, t, HEAD_LANES), lambda b, h, i: (b, i, h)),
        scratch_shapes=[
            pltpu.VMEM((seq, HEAD_LANES), BF16), pltpu.VMEM((seq, HEAD_LANES), BF16),
            pltpu.VMEM((2 * t, t), F32), pltpu.VMEM((2 * t, t), F32), pltpu.VMEM((2 * t, t), BF16),
            pltpu.VMEM((2 * t, HEAD_LANES), F32), pltpu.VMEM((2 * t, HEAD_LANES), F32),
            pltpu.VMEM((2 * t, DIFF_V_DIM), F32),
        ],
        out_shape=jax.ShapeDtypeStruct((batch, seq, DIFF_HEADS * DIFF_V_DIM), BF16),
        compiler_params=_cparams(("arbitrary", "arbitrary", "arbitrary")),
        name="diff_attention",
    )(proj_main, proj_main, proj_main, bias_tiles, lq1, lk1, lq2, lk2, g_subln)


GDN_C = 256
GDN_BASE = 16
GDN_HEADS_PER_STEP = 4


def _softplus(x):
    return jnp.maximum(x, 0.0) + jnp.log1p(jnp.exp(-jnp.abs(x)))


def _silu(x):
    return x * jax.nn.sigmoid(x)


def _mm(a, b):
    return _dot(a.astype(BF16), b.astype(BF16))


def _causal_conv_silu(ref, hh, r0, first, cw):
    cur = ref[0, 0, hh, pl.ds(r0, GDN_C), :]
    prev = ref[0, 0, hh, pl.ds(jnp.maximum(r0 - 8, 0), 8), :]
    prev = jnp.where(first, 0.0, prev)
    ext = jnp.concatenate([prev, cur], axis=0)
    acc = cur * cw[CONV_K - 1:CONV_K, :]
    for k in range(1, CONV_K):
        acc = acc + ext[8 - k:8 - k + GDN_C, :] * cw[CONV_K - 1 - k:CONV_K - k, :]
    return _silu(acc)


def _gdn_kernel(nq_ref, nk_ref, nv_ref, ng_ref, tail_ref, arow_ref, cwq_ref, cwk_ref, cwv_ref,
                alog_ref, dtb_ref, gn_ref, o_ref, gcr_ref):
    c = GDN_C
    n_chunks = nq_ref.shape[3] // c
    ri = lax.broadcasted_iota(jnp.int32, (c, c), 0)
    ci = lax.broadcasted_iota(jnp.int32, (c, c), 1)
    tril = ci <= ri
    strict = ci < ri
    tril_b = _ones_where(tril)
    triu_b = _ones_where(ri <= ci)
    gn = gn_ref[...]

    for hh in range(GDN_HEADS_PER_STEP):
        coef = -jnp.exp(alog_ref[hh])
        g_rows = coef[:, :1] * _softplus(arow_ref[hh, 0] + dtb_ref[hh][:, :1])
        g_rows = jnp.concatenate([g_rows, jnp.zeros((16 - n_chunks, c), F32)], axis=0)
        gcr_ref[hh] = _dot_exact_lhs(g_rows, triu_b)

    heads = range(GDN_HEADS_PER_STEP)

    def each(fn, *per_head):
        return [fn(*args) for args in zip(*per_head)]

    def chunk(n, states):
        r0 = pl.multiple_of(n * c, c)
        first = n == 0
        lanes = [slice(hh * HEAD_LANES, (hh + 1) * HEAD_LANES) for hh in heads]
        coef = [-jnp.exp(alog_ref[hh]) for hh in heads]
        dtb = [dtb_ref[hh] for hh in heads]
        sel_r = lax.broadcasted_iota(jnp.int32, (HEAD_LANES, 2 * HEAD_LANES), 0)
        sel_c = lax.broadcasted_iota(jnp.int32, (HEAD_LANES, 2 * HEAD_LANES), 1)
        hid = [pl.program_id(1) * GDN_HEADS_PER_STEP + hh for hh in heads]
        sel = [_ones_where(sel_r == jnp.where(sel_c < HEAD_LANES, h, DN_HEADS + h)) for h in hid]

        cq = [_causal_conv_silu(nq_ref, hh, r0, first, cwq_ref[:, lanes[hh]]) for hh in heads]
        ck = [_causal_conv_silu(nk_ref, hh, r0, first, cwk_ref[:, lanes[hh]]) for hh in heads]
        cv = [_causal_conv_silu(nv_ref, hh, r0, first, cwv_ref[:, lanes[hh]]) for hh in heads]
        qn = each(lambda a: (a * lax.rsqrt(jnp.sum(a * a, axis=-1, keepdims=True) + EPS))
                  * (DN_HEAD_DIM ** -0.5), cq)
        kn = each(lambda a: a * lax.rsqrt(jnp.sum(a * a, axis=-1, keepdims=True) + EPS), ck)

        tail_blk = tail_ref[pl.ds(r0, c), :]
        ba = each(lambda s_: _dot_exact_lhs(tail_blk, s_), sel)
        beta = each(lambda a: jax.nn.sigmoid(a[:, :HEAD_LANES]), ba)
        g_col = each(lambda a, cf, db: cf * _softplus(a[:, HEAD_LANES:] + db), ba, coef, dtb)
        gc = each(lambda g_: _dot_exact_rhs(tril_b, g_), g_col)
        gl = each(lambda g_: g_[c - 1:c, :], gc)
        eg = each(jnp.exp, gc)
        g_row = [gcr_ref[hh, pl.ds(n, 1), :] for hh in heads]

        kb = each(jnp.multiply, kn, beta)
        vb = each(jnp.multiply, cv, beta)
        kkqk = each(lambda kb_, qn_, kn_: _dot(jnp.concatenate([kb_, qn_], axis=0).astype(BF16),
                                               kn_.astype(BF16), NT), kb, qn, kn)
        decay = each(lambda g_, gr: jnp.exp(jnp.where(tril, jnp.concatenate([g_, g_], axis=1) - gr, -jnp.inf)),
                     gc, g_row)
        lmat = each(lambda kq, dc: jnp.where(strict, kq[:c] * dc, 0.0), kkqk, decay)
        attn = each(lambda kq, dc: kq[c:] * dc, kkqk, decay)

        same_base = (ri // GDN_BASE) == (ci // GDN_BASE)
        eye = jnp.where(ri == ci, 1.0, 0.0)
        p = each(lambda l_: jnp.where(same_base, l_, 0.0), lmat)
        tinv = each(lambda p_: eye - p_, p)
        for _ in range(int(math.log2(GDN_BASE)) - 1):
            p = each(lambda p_: _mm(p_, p_), p)
            tinv = each(lambda t_, p_: t_ + _mm(t_, p_), tinv, p)
        blk = GDN_BASE
        while blk < c:
            off_mask = (ri // (2 * blk) == ci // (2 * blk)) & (ri // blk != ci // blk)
            off = each(lambda l_: jnp.where(off_mask, l_, 0.0), lmat)
            half = each(_mm, tinv, off)
            tinv = each(lambda t_, h_: t_ - _mm(h_, t_), tinv, half)
            blk *= 2
        x = each(lambda t_, vb_, kb_, eg_: _mm(t_, jnp.concatenate([vb_, kb_ * eg_], axis=1)),
                 tinv, vb, kb, eg)

        sb = each(lambda s_: s_.astype(BF16), states)
        v_new = each(lambda x_, s_: x_[:, :HEAD_LANES] - _dot(x_[:, HEAD_LANES:].astype(BF16), s_), x, sb)
        vnb = each(lambda v_: v_.astype(BF16), v_new)
        out = each(lambda qn_, eg_, s_, at_, vn_: _dot((qn_ * eg_).astype(BF16), s_) + _dot(at_.astype(BF16), vn_),
                   qn, eg, sb, attn, vnb)
        k_tail = each(lambda kn_, gl_, g_: (kn_ * jnp.exp(gl_ - g_)).astype(BF16), kn, gl, gc)
        new_states = each(lambda st, gl_, kt, vn_: st * jnp.exp(gl_) + _dot(kt, vn_, ((0,), (0,))),
                          states, gl, k_tail, vnb)

        for hh in heads:
            o = out[hh]
            ms = jnp.mean(o * o, axis=-1, keepdims=True)
            y = (o * lax.rsqrt(ms + EPS)) * gn
            gate = ng_ref[0, 0, hh, pl.ds(r0, c), :]
            o_ref[0, pl.ds(r0, c), lanes[hh]] = (y * _silu(gate)).astype(o_ref.dtype)
        return tuple(new_states)

    state0 = tuple(jnp.zeros((DN_HEAD_DIM, DN_HEAD_DIM), F32) for _ in heads)
    lax.fori_loop(0, n_chunks, chunk, state0)


def _gated_deltanet(proj_main, tail, tail_t, conv_w, a_log_b, dt_bias_b, g_gnorm, batch, seq):
    c = GDN_C
    hps = GDN_HEADS_PER_STEP
    n_chunks = seq // c
    assert n_chunks <= 16 and DN_HEADS % hps == 0
    arow = tail_t.reshape(16, batch, n_chunks, c)
    head_spec = lambda g: pl.BlockSpec((1, 1, hps, seq, HEAD_LANES), lambda b, h: (g, b, h, 0, 0))
    cw_spec = lambda g: pl.BlockSpec((CONV_K, hps * HEAD_LANES), lambda b, h: (0, g * (DN_HEADS // hps) + h))
    vec_spec = pl.BlockSpec((hps, 1, HEAD_LANES), lambda b, h: (h, 0, 0))
    return pl.pallas_call(
        _gdn_kernel,
        grid=(batch, DN_HEADS // hps),
        in_specs=[
            head_spec(3), head_spec(4), head_spec(5), head_spec(6),
            pl.BlockSpec((seq, HEAD_LANES), lambda b, h: (b, 0)),
            pl.BlockSpec((hps, 1, n_chunks, c), lambda b, h: (DN_HEADS // hps + h, b, 0, 0)),
            cw_spec(0), cw_spec(1), cw_spec(2),
            vec_spec, vec_spec,
            pl.BlockSpec((1, DN_HEAD_DIM), lambda b, h: (0, 0)),
        ],
        out_specs=pl.BlockSpec((1, seq, hps * HEAD_LANES), lambda b, h: (b, 0, h)),
        out_shape=jax.ShapeDtypeStruct((batch, seq, DN_HEADS * DN_HEAD_DIM), BF16),
        scratch_shapes=[pltpu.VMEM((hps, 16, c), F32)],
        compiler_params=_cparams(("arbitrary", "arbitrary")),
        name="gated_deltanet",
    )(proj_main, proj_main, proj_main, proj_main, tail, arow, conv_w, conv_w, conv_w,
      a_log_b, dt_bias_b, g_gnorm)


OUT_TM = 512
OUT_TK = 512
EPI_ROWS = 256


def _pack_bf16_halves(x):
    n = x.shape[1] // 2
    bits = pltpu.bitcast(x.astype(BF16).astype(F32), jnp.int32)
    lo = lax.shift_right_logical(bits[:, :n], 16)
    hi = jnp.bitwise_and(bits[:, n:], jnp.int32(-65536))
    return jnp.bitwise_or(hi, lo)


def _unpack_bf16_halves(w):
    lo = pltpu.bitcast(lax.shift_left(w, 16), F32).astype(BF16)
    hi = pltpu.bitcast(jnp.bitwise_and(w, jnp.int32(-65536)), F32).astype(BF16)
    return lo, hi


def _dot_f32(a, b_parts):
    a_hi, a_mid, a_lo = _split3(a)
    b_hi, b_mid, b_lo = b_parts
    acc = _dot(a_hi, b_hi)
    acc += _dot(a_hi, b_mid) + _dot(a_mid, b_hi)
    acc += _dot(a_hi, b_lo) + _dot(a_mid, b_mid) + _dot(a_lo, b_hi)
    return acc


def _outproj_kernel(od_ref, on_ref, w_ref, x_ref, g1_ref, gf_ref, sc_ref, sh_ref, wr_ref, br_ref,
                    x1_ref, hn2_ref, idx_ref, gate_ref, rank_ref, cnt_ref, acc_ref, mh_ref, carry_ref):
    i = pl.program_id(0)
    k = pl.program_id(1)
    nk = pl.num_programs(1)
    tm = OUT_TM

    @pl.when((i == 0) & (k == 0))
    def _():
        carry_ref[...] = jnp.zeros_like(carry_ref)

    lhs = jnp.where(k < nk // 2, od_ref[...], on_ref[...])
    part = _dot(lhs, w_ref[...])

    @pl.when(k == 0)
    def _():
        acc_ref[...] = part

    @pl.when(k > 0)
    def _():
        acc_ref[...] += part

    @pl.when(k == nk - 1)
    def _():
        g1 = g1_ref[0]
        gf = gf_ref[...]
        sc = sc_ref[0]
        sh = sh_ref[0]
        wr_parts = _split3(wr_ref[...])
        br = br_ref[...]
        lane = lax.broadcasted_iota(jnp.int32, (EPI_ROWS, HEAD_LANES), 1)
        for r in range(0, tm, EPI_ROWS):
            rows = slice(r, r + EPI_ROWS)
            x1 = x_ref[rows, :] + g1 * acc_ref[rows, :]
            x1_ref[rows, :] = x1
            hn2 = _norm_modulate(x1, gf, sc, sh)
            hn2_ref[rows, :] = _pack_bf16_halves(hn2)
            logits = _dot_f32(hn2, wr_parts) + br
            l = jnp.where(lane < N_EXPERTS, logits, -jnp.inf)
            vals, idxs = [], []
            for _ in range(TOP_K):
                m = jnp.max(l, axis=-1, keepdims=True)
                ix = jnp.min(jnp.where(l == m, lane, HEAD_LANES), axis=-1, keepdims=True)
                vals.append(m)
                idxs.append(ix)
                l = jnp.where(lane == ix, -jnp.inf, l)
            es = [jnp.exp(v - vals[0]) for v in vals]
            den = (es[0] + es[1]) + (es[2] + es[3])
            idx_o = jnp.zeros((EPI_ROWS, HEAD_LANES), jnp.int32)
            gate_o = jnp.zeros((EPI_ROWS, HEAD_LANES), F32)
            hot = jnp.zeros((EPI_ROWS, HEAD_LANES), F32)
            for kk in range(TOP_K):
                idx_o = jnp.where(lane == kk, idxs[kk], idx_o)
                gate_o = jnp.where(lane == kk, es[kk] / den, gate_o)
                hot = jnp.where(lane == idxs[kk], 1.0, hot)
            idx_ref[rows, :] = idx_o
            gate_ref[rows, :] = gate_o
            mh_ref[rows, :] = hot.astype(BF16)

        ri = lax.broadcasted_iota(jnp.int32, (tm, tm), 0)
        ci = lax.broadcasted_iota(jnp.int32, (tm, tm), 1)
        mh = mh_ref[...]
        carry = carry_ref[0:1, :]
        cum = _dot(_ones_where(ci < ri), mh) + carry
        lane_t = lax.broadcasted_iota(jnp.int32, (tm, HEAD_LANES), 1)
        idx_t = idx_ref[...]
        rank_o = jnp.zeros((tm, HEAD_LANES), F32)
        for kk in range(TOP_K):
            ik = idx_t[:, kk:kk + 1]
            rk = jnp.sum(jnp.where(lane_t == ik, cum, 0.0), axis=-1, keepdims=True)
            rank_o = jnp.where(lane_t == kk, rk, rank_o)
        rank_ref[...] = rank_o.astype(jnp.int32)
        total = carry + jnp.sum(mh.astype(F32), axis=0, keepdims=True)
        carry_ref[...] = jnp.broadcast_to(total, carry_ref.shape)
        cnt_ref[...] = jnp.broadcast_to(total, cnt_ref.shape).astype(jnp.int32)


def _outproj_router(o_diff, o_dn, w_out, x2, gate1, g_ffn, scale2, shift2, w_router_pad, b_router_pad, seq):
    n_tok, d = x2.shape
    tm, tk = OUT_TM, OUT_TK
    tiles_per_batch = seq // tm
    nk = d // tk
    half = nk // 2
    vec3 = pl.BlockSpec((1, 1, d), lambda i, k: (i // tiles_per_batch, 0, 0))
    row_out = lambda dt: jax.ShapeDtypeStruct((n_tok, HEAD_LANES), dt)
    return pl.pallas_call(
        _outproj_kernel,
        grid=(n_tok // tm, nk),
        in_specs=[
            pl.BlockSpec((tm, tk), lambda i, k: (i, jnp.minimum(k, half - 1))),
            pl.BlockSpec((tm, tk), lambda i, k: (i, jnp.maximum(k - half, 0))),
            pl.BlockSpec((tk, d), lambda i, k: (k, 0)),
            pl.BlockSpec((tm, d), lambda i, k: (i, 0)),
            vec3,
            pl.BlockSpec((1, d), lambda i, k: (0, 0)),
            vec3, vec3,
            pl.BlockSpec((d, HEAD_LANES), lambda i, k: (0, 0)),
            pl.BlockSpec((1, HEAD_LANES), lambda i, k: (0, 0)),
        ],
        out_specs=[
            pl.BlockSpec((tm, d), lambda i, k: (i, 0)),
            pl.BlockSpec((tm, d // 2), lambda i, k: (i, 0)),
            pl.BlockSpec((tm, HEAD_LANES), lambda i, k: (i, 0)),
            pl.BlockSpec((tm, HEAD_LANES), lambda i, k: (i, 0)),
            pl.BlockSpec((tm, HEAD_LANES), lambda i, k: (i, 0)),
            pl.BlockSpec((8, HEAD_LANES), lambda i, k: (0, 0)),
        ],
        out_shape=[
            jax.ShapeDtypeStruct((n_tok, d), F32),
            jax.ShapeDtypeStruct((n_tok, d // 2), jnp.int32),
            row_out(jnp.int32), row_out(F32), row_out(jnp.int32),
            jax.ShapeDtypeStruct((8, HEAD_LANES), jnp.int32),
        ],
        scratch_shapes=[pltpu.VMEM((tm, d), F32), pltpu.VMEM((tm, HEAD_LANES), BF16),
                        pltpu.VMEM((8, HEAD_LANES), F32)],
        compiler_params=_cparams(("arbitrary", "arbitrary")),
        name="outproj_router",
    )(o_diff, o_dn, w_out, x2, gate1, g_ffn, scale2, shift2, w_router_pad, b_router_pad)


MOE_TM = 2048
MOE_SUB = 256
MOE_CHUNK = 512
MOE_TF = 512
MOE_TN = 512
DSP_TM = 256


def _row_copy(src_hbm, row, dst_vmem, slot, sem):
    return pltpu.make_async_copy(src_hbm.at[pl.ds(row, 1)], dst_vmem.at[pl.ds(slot, 1)], sem)


def _dispatch_kernel(sbr_ref, sbm_ref, dest_ref, hn2_ref, xs_hbm, zero_ref, sem):
    i = pl.program_id(0)
    tm = DSP_TM

    @pl.when(i == 0)
    def _():
        zero_ref[...] = jnp.zeros_like(zero_ref)
        n_sb = sbr_ref.shape[0]

        def pad_copies(s, fn):
            nv = sbr_ref[s]
            nv8 = (nv + 7) // 8 * 8
            for t in range(7):
                @pl.when(nv + t < nv8)
                def _():
                    fn(pltpu.make_async_copy(zero_ref.at[pl.ds(0, 1)],
                                             xs_hbm.at[pl.ds(s * MOE_TM + nv + t, 1)], sem))

            pad8 = (sbm_ref[s] - nv8) // 8
            bit = MOE_SUB // 16
            while bit >= 1:
                rows = 8 * bit
                start = pl.multiple_of(s * MOE_TM + nv8 + 8 * (pad8 & ~(2 * bit - 1)), 8)

                @pl.when((pad8 & bit) != 0)
                def _():
                    fn(pltpu.make_async_copy(zero_ref.at[pl.ds(0, rows)],
                                             xs_hbm.at[pl.ds(start, rows)], sem))

                bit //= 2
            return 0

        lax.fori_loop(0, n_sb, lambda s, c: c + pad_copies(s, lambda cp: cp.start()), 0)
        lax.fori_loop(0, n_sb, lambda s, c: c + pad_copies(s, lambda cp: cp.wait()), 0)

    def issue(r, carry):
        for kk in range(TOP_K):
            pltpu.make_async_copy(hn2_ref.at[pl.ds(r, 1)],
                                  xs_hbm.at[pl.ds(dest_ref[0, 0, kk * tm + r], 1)], sem).start(priority=kk % 2)
        return carry

    lax.fori_loop(0, tm, issue, 0)

    def wait(r, carry):
        for kk in range(TOP_K):
            pltpu.make_async_copy(hn2_ref.at[pl.ds(0, 1)], xs_hbm.at[pl.ds(0, 1)], sem).wait()
        return carry

    lax.fori_loop(0, tm, wait, 0)


def _dispatch(sb_rows, sb_m, dest_tiles, hn2, n_rows):
    n_tok, d = hn2.shape
    tm = DSP_TM
    grid_spec = pltpu.PrefetchScalarGridSpec(
        num_scalar_prefetch=2,
        grid=(n_tok // tm,),
        in_specs=[
            pl.BlockSpec((1, 1, tm * TOP_K), lambda i, sbr, sbm: (i, 0, 0), memory_space=pltpu.SMEM),
            pl.BlockSpec((tm, d), lambda i, sbr, sbm: (i, 0)),
        ],
        out_specs=pl.BlockSpec(memory_space=pl.ANY),
        scratch_shapes=[pltpu.VMEM((MOE_SUB // 2, d), hn2.dtype), pltpu.SemaphoreType.DMA(())],
    )
    return pl.pallas_call(
        _dispatch_kernel,
        grid_spec=grid_spec,
        out_shape=jax.ShapeDtypeStruct((n_rows, d), hn2.dtype),
        compiler_params=_cparams(("arbitrary",)),
        name="moe_dispatch",
    )(sb_rows, sb_m, dest_tiles, hn2)


def _for_row_chunks(m, body):
    n_full = m // MOE_CHUNK

    def step(c, carry):
        body(pl.multiple_of(c * MOE_CHUNK, MOE_CHUNK), MOE_CHUNK)
        return carry

    lax.fori_loop(0, n_full, step, 0)
    assert MOE_CHUNK == 2 * MOE_SUB

    @pl.when(m - n_full * MOE_CHUNK > 0)
    def _():
        body(pl.multiple_of(n_full * MOE_CHUNK, MOE_CHUNK), MOE_SUB)


def _moe_up_kernel(sbe_ref, sbm_ref, nused_ref, x_ref, wg_ref, wu_ref, bg_ref, bu_ref, h_ref,
                   xb_ref, wgb_ref, wub_ref):
    s = pl.program_id(0)
    f = pl.program_id(1)
    m = sbm_ref[s]
    half = x_ref.shape[1]

    @pl.when(f == 0)
    def _():
        def unpack(r0, rows):
            lo, hi = _unpack_bf16_halves(x_ref[pl.ds(r0, rows), :])
            xb_ref[pl.ds(r0, rows), 0:half] = lo
            xb_ref[pl.ds(r0, rows), half:2 * half] = hi

        _for_row_chunks(m, unpack)

    @pl.when(m > 0)
    def _():
        wgb_ref[...] = wg_ref[0].astype(BF16)
        wub_ref[...] = wu_ref[0].astype(BF16)

    def compute(r0, rows):
        xq = xb_ref[pl.ds(r0, rows), :]
        gate = jnp.minimum(_dot(xq, wgb_ref[...]) + bg_ref[0], SWIGLU_LIMIT)
        up = jnp.clip(_dot(xq, wub_ref[...]) + bu_ref[0], -SWIGLU_LIMIT, SWIGLU_LIMIT)
        glu = gate * jax.nn.sigmoid(gate * SWIGLU_ALPHA)
        h_ref[pl.ds(r0, rows), :] = ((up + 1.0) * glu).astype(BF16)

    _for_row_chunks(m, compute)


def _moe_up(sb_expert, sb_m, n_used, xs, w_gate_up, b_gate_up):
    n_sb = sb_expert.shape[0]
    d = xs.shape[1]
    n_exp, dm, f2 = w_gate_up.shape
    assert dm == 2 * d
    f_dim = f2 // 2
    nf = f_dim // MOE_TF

    def f_eff(s, f, nu):
        return jnp.where(s < nu[0], f, nf - 1)

    def s_eff(s, nu):
        return jnp.minimum(s, nu[0] - 1)

    grid_spec = pltpu.PrefetchScalarGridSpec(
        num_scalar_prefetch=3,
        grid=(n_sb, nf),
        in_specs=[
            pl.BlockSpec((MOE_TM, d), lambda s, f, sbe, sbm, nu: (s_eff(s, nu), 0)),
            pl.BlockSpec((1, dm, MOE_TF), lambda s, f, sbe, sbm, nu: (sbe[s], 0, f_eff(s, f, nu))),
            pl.BlockSpec((1, dm, MOE_TF), lambda s, f, sbe, sbm, nu: (sbe[s], 0, nf + f_eff(s, f, nu))),
            pl.BlockSpec((1, 1, MOE_TF), lambda s, f, sbe, sbm, nu: (sbe[s], 0, f_eff(s, f, nu))),
            pl.BlockSpec((1, 1, MOE_TF), lambda s, f, sbe, sbm, nu: (sbe[s], 0, nf + f_eff(s, f, nu))),
        ],
        out_specs=pl.BlockSpec((MOE_TM, MOE_TF), lambda s, f, sbe, sbm, nu: (s_eff(s, nu), f_eff(s, f, nu))),
        scratch_shapes=[pltpu.VMEM((MOE_TM, 2 * d), BF16), pltpu.VMEM((2 * d, MOE_TF), BF16),
                        pltpu.VMEM((2 * d, MOE_TF), BF16)],
    )
    b3 = b_gate_up.reshape(n_exp, 1, f2)
    return pl.pallas_call(
        _moe_up_kernel,
        grid_spec=grid_spec,
        out_shape=jax.ShapeDtypeStruct((n_sb * MOE_TM, f_dim), BF16),
        compiler_params=_cparams(("arbitrary", "arbitrary")),
        name="moe_up",
    )(sb_expert, sb_m, n_used, xs, w_gate_up, w_gate_up, b3, b3)


def _for_static_rows(m, body):
    n_full = m // MOE_CHUNK
    for n in range(1, MOE_TM // MOE_CHUNK + 1):
        @pl.when(n_full == n)
        def _():
            body(0, n * MOE_CHUNK)

    assert MOE_CHUNK == 2 * MOE_SUB

    @pl.when(m - n_full * MOE_CHUNK > 0)
    def _():
        body(pl.multiple_of(n_full * MOE_CHUNK, MOE_CHUNK), MOE_SUB)


def _moe_down_kernel(sbe_ref, sbm_ref, nused_ref, h_ref, wd_ref, bd_ref, o_ref):
    s = pl.program_id(0)
    m = sbm_ref[s]

    def compute(r0, rows):
        o_ref[pl.ds(r0, rows), :] = _dot(h_ref[pl.ds(r0, rows), :], wd_ref[0].astype(BF16)) + bd_ref[0]

    _for_static_rows(m, compute)


def _moe_down(sb_expert, sb_m, n_used, h, w_down, b_down):
    n_sb = sb_expert.shape[0]
    n_exp, f_dim, d = w_down.shape
    nn = d // MOE_TN

    def n_eff(s, n, nu):
        return jnp.where(s < nu[0], n, nn - 1)

    def s_eff(s, nu):
        return jnp.minimum(s, nu[0] - 1)

    grid_spec = pltpu.PrefetchScalarGridSpec(
        num_scalar_prefetch=3,
        grid=(n_sb, nn),
        in_specs=[
            pl.BlockSpec((MOE_TM, f_dim), lambda s, n, sbe, sbm, nu: (s_eff(s, nu), 0)),
            pl.BlockSpec((1, f_dim, MOE_TN), lambda s, n, sbe, sbm, nu: (sbe[s], 0, n_eff(s, n, nu))),
            pl.BlockSpec((1, 1, MOE_TN), lambda s, n, sbe, sbm, nu: (sbe[s], 0, n_eff(s, n, nu))),
        ],
        out_specs=pl.BlockSpec((MOE_TM, MOE_TN), lambda s, n, sbe, sbm, nu: (s_eff(s, nu), n_eff(s, n, nu))),
    )
    return pl.pallas_call(
        _moe_down_kernel,
        grid_spec=grid_spec,
        out_shape=jax.ShapeDtypeStruct((n_sb * MOE_TM, d), F32),
        compiler_params=_cparams(("arbitrary", "arbitrary")),
        name="moe_down",
    )(sb_expert, sb_m, n_used, h, w_down, b_down.reshape(n_exp, 1, d))


CMB_TM = 256


def _combine_kernel(dest_ref, ys_hbm, x1_ref, gate_ref, g2_ref, gfin_ref, o_ref, buf_ref, sem):
    tm = CMB_TM

    def issue(r, carry):
        for kk in range(TOP_K):
            _row_copy(ys_hbm, dest_ref[0, 0, kk * tm + r], buf_ref.at[kk], r, sem).start(priority=kk % 2)
        return carry

    lax.fori_loop(0, tm, issue, 0)

    def wait(r, carry):
        for kk in range(TOP_K):
            _row_copy(ys_hbm, 0, buf_ref.at[kk], 0, sem).wait()
        return carry

    lax.fori_loop(0, tm, wait, 0)

    gates = gate_ref[...]
    y = gates[:, 0:1] * buf_ref[0]
    for kk in range(1, TOP_K):
        y = y + gates[:, kk:kk + 1] * buf_ref[kk]
    x2 = x1_ref[...] + g2_ref[0] * y
    ms = jnp.mean(x2 * x2, axis=-1, keepdims=True)
    o_ref[...] = (x2 * lax.rsqrt(ms + EPS)) * gfin_ref[...]


def _combine(dest_tiles, ys, x1, gates, gate2, g_final, seq):
    n_tok, d = x1.shape
    tm = CMB_TM
    tiles_per_batch = seq // tm
    return pl.pallas_call(
        _combine_kernel,
        grid=(n_tok // tm,),
        in_specs=[
            pl.BlockSpec((1, 1, tm * TOP_K), lambda i: (i, 0, 0), memory_space=pltpu.SMEM),
            pl.BlockSpec(memory_space=pl.ANY),
            pl.BlockSpec((tm, d), lambda i: (i, 0)),
            pl.BlockSpec((tm, HEAD_LANES), lambda i: (i, 0)),
            pl.BlockSpec((1, 1, d), lambda i: (i // tiles_per_batch, 0, 0)),
            pl.BlockSpec((1, d), lambda i: (0, 0)),
        ],
        out_specs=pl.BlockSpec((tm, d), lambda i: (i, 0)),
        out_shape=jax.ShapeDtypeStruct((n_tok, d), F32),
        scratch_shapes=[pltpu.VMEM((TOP_K, tm, d), F32), pltpu.SemaphoreType.DMA(())],
        compiler_params=_cparams(("arbitrary",)),
        name="moe_combine_norm",
    )(dest_tiles, ys, x1, gates, gate2, g_final)


def _route_tables(top_idx, rank, counts, n_tok):
    n_sb = n_tok * TOP_K // MOE_TM + N_EXPERTS
    max_sb_per_e = n_tok // MOE_TM
    cnt = counts[0, :N_EXPERTS]
    sb_per_e = (cnt + MOE_TM - 1) // MOE_TM
    rows_per_sb = (cnt + jnp.maximum(sb_per_e, 1) - 1) // jnp.maximum(sb_per_e, 1)
    sb_end = jnp.cumsum(sb_per_e)
    sb_start = sb_end - sb_per_e
    n_used = sb_end[-1]
    s_ids = jnp.arange(n_sb, dtype=jnp.int32)
    e_ids = jnp.arange(N_EXPERTS, dtype=jnp.int32)
    s_clamped = jnp.minimum(s_ids, n_used - 1)
    e_of_s = jnp.sum((sb_end[None, :] <= s_clamped[:, None]).astype(jnp.int32), axis=1)
    e_of_s = jnp.minimum(e_of_s, N_EXPERTS - 1).astype(jnp.int32)
    onehot_s = (e_of_s[:, None] == e_ids[None, :]).astype(jnp.int32)
    j_of_s = s_ids - jnp.sum(onehot_s * sb_start[None, :], axis=1)
    per_s = jnp.sum(onehot_s * rows_per_sb[None, :], axis=1)
    cnt_s = jnp.sum(onehot_s * cnt[None, :], axis=1)
    sb_rows = jnp.where(s_ids < n_used, jnp.clip(cnt_s - j_of_s * per_s, 0, per_s), 0).astype(jnp.int32)
    sb_m = ((sb_rows + MOE_SUB - 1) // MOE_SUB * MOE_SUB).astype(jnp.int32)
    idx_t = top_idx[:, :TOP_K].T
    rank_t = rank[:, :TOP_K].T
    onehot_a = idx_t[None, :, :] == e_ids[:, None, None]
    start_a = jnp.sum(jnp.where(onehot_a, sb_start[:, None, None], 0), axis=0)
    per_a = jnp.sum(jnp.where(onehot_a, rows_per_sb[:, None, None], 0), axis=0)
    sb_in_e = sum((rank_t >= j * per_a).astype(jnp.int32) for j in range(1, max_sb_per_e))
    dest_t = ((start_a + sb_in_e) * MOE_TM + (rank_t - sb_in_e * per_a)).astype(jnp.int32)
    return e_of_s, sb_rows, sb_m, n_used.reshape(1).astype(jnp.int32), dest_t, n_sb


def _moe_block(hn2_packed, top_idx, rank, counts, x1, gates, gate2, w_gate_up, b_gate_up, w_down, b_down,
               g_final, seq):
    n_tok = x1.shape[0]
    e_of_s, sb_rows, sb_m, n_used1, dest_t, n_sb = _route_tables(top_idx, rank, counts, n_tok)

    def dest_tiles(tm):
        return dest_t.reshape(TOP_K, n_tok // tm, tm).transpose(1, 0, 2).reshape(n_tok // tm, 1, TOP_K * tm)

    xs = _dispatch(sb_rows, sb_m, dest_tiles(DSP_TM), hn2_packed, n_sb * MOE_TM)
    hmid = _moe_up(e_of_s, sb_m, n_used1, xs, w_gate_up, b_gate_up)
    ys = _moe_down(e_of_s, sb_m, n_used1, hmid, w_down, b_down)
    return _combine(dest_tiles(CMB_TM), ys, x1, gates, gate2, g_final[None, :], seq)


def kernel(x, c, w_ada, b_ada, g_norm_mix, g_norm_ffn, g_norm_final, w_in, conv_w, rel_bias,
           lambda_q1, lambda_k1, lambda_q2, lambda_k2, g_subln, a_log, dt_bias, g_gnorm, w_out,
           w_router, b_router, w_gate_up, b_gate_up, w_down, b_down):
    batch, seq, d = x.shape
    n_tok = batch * seq
    depth = w_ada.shape[0]
    assert depth == 1 and d == D_MODEL
    x2 = x.reshape(n_tok, d)
    bias_tiles = _bias_tiles(rel_bias)
    for l in range(depth):
        c_pad = jnp.zeros((16, d), F32).at[:batch].set(c)
        mod = _adaln(c_pad, w_ada[l], b_ada[l][None, :])[:batch]
        shift1, scale1, gate1, shift2, scale2, gate2 = [m.reshape(batch, 1, d) for m in jnp.split(mod, 6, axis=-1)]

        w_tail = jnp.pad(w_in[l, :, MAIN_COLS:], ((0, 0), (0, HEAD_LANES - 2 * DN_HEADS)))
        w_tail_t = w_in[l, :, MAIN_COLS:].T
        proj_main, tail, tail_t = _inproj(x2, g_norm_mix[l][None, :], scale1, shift1,
                                          w_in[l, :, :MAIN_COLS].astype(BF16), w_tail, w_tail_t, batch, seq)
        o_diff = _diff_attention(proj_main, bias_tiles, lambda_q1[l][None, :], lambda_k1[l][None, :],
                                 lambda_q2[l][None, :], lambda_k2[l][None, :], g_subln[l][None, :], batch, seq)
        a_log_b = jnp.broadcast_to(a_log[l][:, None, None], (DN_HEADS, 1, HEAD_LANES))
        dt_bias_b = jnp.broadcast_to(dt_bias[l][:, None, None], (DN_HEADS, 1, HEAD_LANES))
        o_dn = _gated_deltanet(proj_main, tail, tail_t, conv_w[l], a_log_b, dt_bias_b, g_gnorm[l][None, :],
                               batch, seq)

        w_router_pad = jnp.pad(w_router[l], ((0, 0), (0, HEAD_LANES - N_EXPERTS)))
        b_router_pad = jnp.pad(b_router[l], (0, HEAD_LANES - N_EXPERTS))[None, :]
        x1, hn2, top_idx, gates, rank, counts = _outproj_router(
            o_diff.reshape(n_tok, -1), o_dn.reshape(n_tok, -1), w_out[l].astype(BF16), x2, gate1, g_norm_ffn[l][None, :],
            scale2, shift2, w_router_pad, b_router_pad, seq)

        x2 = _moe_block(hn2, top_idx, rank, counts, x1, gates, gate2, w_gate_up[l], b_gate_up[l], w_down[l],
                        b_down[l], g_norm_final, seq)
    return x2.reshape(batch, seq, d)
```
